```python
import math
import jax, jax.numpy as jnp
from jax import lax
import numpy as np

D_MODEL = 1024
BATCH = 16
SEQ = 2048
DEPTH = 4

HEAD_DIM = 64
EPS = 1e-6
Q_BLOCK = 128
MOBA_HEADS = D_MODEL // 2 // HEAD_DIM
FOX_HEADS = D_MODEL // 2 // HEAD_DIM
MOBA_BLOCK = 256
MOBA_TOPK = 3
MOBA_Q_CHUNK = 16
DIFF_V_DIM = 2 * HEAD_DIM
DIFF_HEADS = D_MODEL // 2 // DIFF_V_DIM
SWA_Q_HEADS = D_MODEL // 2 // HEAD_DIM
SWA_KV_HEADS = SWA_Q_HEADS // 4
SWA_WINDOW = 128

N_AB = (DEPTH + 1) // 2
N_CD = DEPTH // 2

MOBA_W = MOBA_HEADS * HEAD_DIM
FOX_W = FOX_HEADS * HEAD_DIM
AB_SIZES = [MOBA_W, MOBA_W, MOBA_W, MOBA_W, FOX_W, FOX_W, FOX_W, FOX_W, FOX_HEADS]
AB_IN = sum(AB_SIZES)
DIFF_QK_W = DIFF_HEADS * 2 * HEAD_DIM
DIFF_W = DIFF_HEADS * DIFF_V_DIM
SWA_W = SWA_Q_HEADS * HEAD_DIM
SWA_KV_W = SWA_KV_HEADS * HEAD_DIM
CD_SIZES = [DIFF_QK_W, DIFF_QK_W, DIFF_W, DIFF_W, SWA_W, SWA_KV_W, SWA_KV_W, SWA_W]
CD_IN = sum(CD_SIZES)
MIX_W = D_MODEL

kernel_name = "hybrid_moba_fox_diff_swa_gated"


def rmsnorm(x, gain):
    xf = x.astype(jnp.float32)
    y = xf * lax.rsqrt(jnp.mean(xf * xf, axis=-1, keepdims=True) + EPS)
    return (y * gain.astype(jnp.float32)).astype(x.dtype)


def alibi_slopes(n):
    return jnp.asarray(2.0 ** (-8.0 * np.arange(1, n + 1) / n), dtype=jnp.float32)


def split_cols(t, sizes):
    idx = np.cumsum(sizes)[:-1].tolist()
    return jnp.split(t, idx, axis=-1)


def to_heads(t, n):
    b, s, _ = t.shape
    return t.reshape(b, s, n, -1).transpose(0, 2, 1, 3)


def from_heads(t):
    b, h, s, d = t.shape
    return t.transpose(0, 2, 1, 3).reshape(b, s, h * d)


def moba_attention(q, k, v, slopes):
    B, H, S, Dh = q.shape
    L = MOBA_BLOCK
    nb = -(-S // L)
    pad = nb * L - S
    kp = jnp.pad(k, ((0, 0), (0, 0), (0, pad), (0, 0)))
    vp = jnp.pad(v, ((0, 0), (0, 0), (0, pad), (0, 0)))
    kb = kp.reshape(B, H, nb, L, Dh)
    vb = vp.reshape(B, H, nb, L, Dh)
    kmean = jnp.mean(kb.astype(jnp.float32), axis=3)
    pos = jnp.arange(S)
    qblk = pos // L
    gate = jnp.einsum('bhsd,bhnd->bhsn', q.astype(jnp.float32), kmean)
    past = jnp.arange(nb)[None, :] < qblk[:, None]
    gate = jnp.where(past, gate, -jnp.inf)
    topk = min(MOBA_TOPK, nb)
    _, sel = lax.top_k(gate, topk)
    sel_valid = sel < qblk[None, None, :, None]
    scale = Dh ** -0.5
    bi = jnp.arange(B)[:, None, None, None]
    hi = jnp.arange(H)[None, :, None, None]
    C = MOBA_Q_CHUNK

    def chunk(c):
        start = c * C
        qc = lax.dynamic_slice_in_dim(q, start, C, axis=2)
        selc = lax.dynamic_slice_in_dim(sel, start, C, axis=2)
        validc = lax.dynamic_slice_in_dim(sel_valid, start, C, axis=2)
        tpos = start + jnp.arange(C)
        kg = kb[bi, hi, selc]
        vg = vb[bi, hi, selc]
        s_past = jnp.einsum('bhqd,bhqnld->bhqnl', qc, kg).astype(jnp.float32) * scale
        spos = selc[..., None] * L + jnp.arange(L)
        rel_p = (tpos[None, None, :, None, None] - spos).astype(jnp.float32)
        s_past = jnp.where(validc[..., None],
                           s_past - slopes[None, :, None, None, None] * rel_p, -jnp.inf)
        s_past = s_past.reshape(B, H, C, topk * L)
        own_start = (start // L) * L
        ko = lax.dynamic_slice_in_dim(kp, own_start, L, axis=2)
        vo = lax.dynamic_slice_in_dim(vp, own_start, L, axis=2)
        s_own = jnp.einsum('bhqd,bhld->bhql', qc, ko).astype(jnp.float32) * scale
        rel_o = tpos[:, None] - (own_start + jnp.arange(L))[None, :]
        s_own = jnp.where(rel_o >= 0,
                          s_own - slopes[None, :, None, None] * rel_o.astype(jnp.float32),
                          -jnp.inf)
        p = jax.nn.softmax(jnp.concatenate([s_past, s_own], axis=-1), axis=-1).astype(v.dtype)
        p_past = p[..., :topk * L].reshape(B, H, C, topk, L)
        p_own = p[..., topk * L:]
        return (jnp.einsum('bhqnl,bhqnld->bhqd', p_past, vg)
                + jnp.einsum('bhql,bhld->bhqd', p_own, vo))

    outs = lax.map(chunk, jnp.arange(S // C))
    return outs.transpose(1, 2, 0, 3, 4).reshape(B, H, S, Dh)


def fox_attention(q, k, v, log_f):
    B, H, S, Dh = q.shape
    c = jnp.cumsum(log_f, axis=-1)
    scale = Dh ** -0.5
    kpos = jnp.arange(S)

    def block(i):
        start = i * Q_BLOCK
        qb = lax.dynamic_slice_in_dim(q, start, Q_BLOCK, axis=2)
        cb = lax.dynamic_slice_in_dim(c, start, Q_BLOCK, axis=2)
        s = (jnp.einsum('bhqd,bhsd->bhqs', qb, k).astype(jnp.float32) * scale
             + cb[..., None] - c[:, :, None, :])
        tpos = start + jnp.arange(Q_BLOCK)
        s = jnp.where(tpos[:, None] >= kpos[None, :], s, -jnp.inf)
        p = jax.nn.softmax(s, axis=-1).astype(v.dtype)
        return jnp.einsum('bhqs,bhsd->bhqd', p, v)

    outs = lax.map(block, jnp.arange(S // Q_BLOCK))
    return outs.transpose(1, 2, 0, 3, 4).reshape(B, H, S, Dh)


def diff_attention(q, k, v, lam, slopes):
    B, H, S, _, Dh = q.shape
    scale = Dh ** -0.5
    kpos = jnp.arange(S)

    def block(i):
        start = i * Q_BLOCK
        qb = lax.dynamic_slice_in_dim(q, start, Q_BLOCK, axis=2)
        s = jnp.einsum('bhqcd,bhscd->bhcqs', qb, k).astype(jnp.float32) * scale
        rel = (start + jnp.arange(Q_BLOCK))[:, None] - kpos[None, :]
        s = jnp.where(rel >= 0,
                      s - slopes[None, :, None, None, None] * rel.astype(jnp.float32),
                      -jnp.inf)
        p = jax.nn.softmax(s, axis=-1)
        a = (p[:, :, 0] - lam * p[:, :, 1]).astype(v.dtype)
        return jnp.einsum('bhqs,bhse->bhqe', a, v)

    outs = lax.map(block, jnp.arange(S // Q_BLOCK))
    return outs.transpose(1, 2, 0, 3, 4).reshape(B, H, S, v.shape[-1])


def swa_attention(q, k, v, sinks, slopes):
    B, S, Hq, Dh = q.shape
    Hkv = k.shape[2]
    G = Hq // Hkv
    nb = S // Q_BLOCK
    qb = q.reshape(B, nb, Q_BLOCK, Hkv, G, Dh)
    kb = k.reshape(B, nb, Q_BLOCK, Hkv, Dh)
    vb = v.reshape(B, nb, Q_BLOCK, Hkv, Dh)
    pad = ((0, 0), (1, 0), (0, 0), (0, 0), (0, 0))
    kband = jnp.concatenate([jnp.pad(kb, pad)[:, :-1], kb], axis=2)
    vband = jnp.concatenate([jnp.pad(vb, pad)[:, :-1], vb], axis=2)
    s = jnp.einsum('bnqgrd,bnkgd->bngrqk', qb, kband).astype(jnp.float32) * (Dh ** -0.5)
    qi = jnp.arange(Q_BLOCK)
    ki = jnp.arange(2 * Q_BLOCK)
    rel = (Q_BLOCK + qi)[:, None] - ki[None, :]
    in_win = (rel >= 0) & (rel < SWA_WINDOW)
    has_prev = (jnp.arange(nb) > 0)[:, None, None] | (ki >= Q_BLOCK)[None, None, :]
    mask = in_win[None] & has_prev
    bias = -slopes.reshape(Hkv, G)[:, :, None, None] * rel.astype(jnp.float32)
    s = jnp.where(mask[None, :, None, None], s + bias, -jnp.inf)
    sink = jnp.broadcast_to(
        sinks.astype(jnp.float32).reshape(Hkv, G)[None, None, :, :, None, None],
        s.shape[:-1] + (1,))
    p = jax.nn.softmax(jnp.concatenate([s, sink], axis=-1), axis=-1)[..., :-1]
    out = jnp.einsum('bngrqk,bnkgd->bnqgrd', p.astype(v.dtype), vband)
    return out.reshape(B, S, Hq * Dh)


def setup_inputs(seed: int = 0) -> dict:
    key = jax.random.key(seed)
    ks = jax.random.split(key, 20)
    f32 = jnp.float32
    nrm = lambda k, shp: jax.random.normal(k, shp, f32)
    gain = lambda k, shp: 1.0 + 0.05 * nrm(k, shp)
    return {
        "x": nrm(ks[0], (BATCH, SEQ, D_MODEL)),
        "norm_gain": gain(ks[1], (DEPTH, D_MODEL)),
        "w_in_ab": nrm(ks[2], (N_AB, D_MODEL, AB_IN)) * D_MODEL ** -0.5,
        "b_forget": 3.0 + 0.3 * nrm(ks[3], (N_AB, FOX_HEADS)),
        "moba_q_gain": gain(ks[4], (N_AB, HEAD_DIM)),
        "moba_k_gain": gain(ks[5], (N_AB, HEAD_DIM)),
        "fox_q_gain": gain(ks[6], (N_AB, HEAD_DIM)),
        "fox_k_gain": gain(ks[7], (N_AB, HEAD_DIM)),
        "w_out_ab": nrm(ks[8], (N_AB, MIX_W, D_MODEL)) * MIX_W ** -0.5,
        "w_in_cd": nrm(ks[9], (N_CD, D_MODEL, CD_IN)) * D_MODEL ** -0.5,
        "diff_q_gain": gain(ks[10], (N_CD, HEAD_DIM)),
        "diff_k_gain": gain(ks[11], (N_CD, HEAD_DIM)),
        "diff_lambda": 0.1 * nrm(ks[12], (N_CD, 4, HEAD_DIM)),
        "diff_subln_gain": gain(ks[13], (N_CD, DIFF_V_DIM)),
        "swa_q_gain": gain(ks[14], (N_CD, HEAD_DIM)),
        "swa_k_gain": gain(ks[15], (N_CD, HEAD_DIM)),
        "swa_sinks": nrm(ks[16], (N_CD, SWA_Q_HEADS)),
        "w_out_cd": nrm(ks[17], (N_CD, MIX_W, D_MODEL)) * MIX_W ** -0.5,
    }


def reference(x, norm_gain, w_in_ab, b_forget, moba_q_gain, moba_k_gain, fox_q_gain,
              fox_k_gain, w_out_ab, w_in_cd, diff_q_gain, diff_k_gain, diff_lambda,
              diff_subln_gain, swa_q_gain, swa_k_gain, swa_sinks, w_out_cd):
    B, S, _ = x.shape
    moba_slopes = alibi_slopes(MOBA_HEADS)
    diff_slopes = alibi_slopes(DIFF_HEADS)
    swa_slopes = alibi_slopes(SWA_Q_HEADS)
    for layer in range(DEPTH):
        h = rmsnorm(x, norm_gain[layer])
        j = layer // 2
        if layer % 2 == 0:
            proj = h @ w_in_ab[j]
            qa, ka, va, ga, qb, kb_, vb_, gb, fb = split_cols(proj, AB_SIZES)
            qa = rmsnorm(to_heads(qa, MOBA_HEADS), moba_q_gain[j])
            ka = rmsnorm(to_heads(ka, MOBA_HEADS), moba_k_gain[j])
            ya = from_heads(moba_attention(qa, ka, to_heads(va, MOBA_HEADS), moba_slopes))
            qb = rmsnorm(to_heads(qb, FOX_HEADS), fox_q_gain[j])
            kb_ = rmsnorm(to_heads(kb_, FOX_HEADS), fox_k_gain[j])
            log_f = jax.nn.log_sigmoid(fb.astype(jnp.float32)
                                       + b_forget[j].astype(jnp.float32)).transpose(0, 2, 1)
            yb = from_heads(fox_attention(qb, kb_, to_heads(vb_, FOX_HEADS), log_f))
            y = jnp.concatenate([ya * jax.nn.silu(ga), yb * jax.nn.silu(gb)], axis=-1)
            x = x + y @ w_out_ab[j]
        else:
            proj = h @ w_in_cd[j]
            qc, kc, vc, gc, qd, kd, vd, gd = split_cols(proj, CD_SIZES)
            qc = rmsnorm(qc.reshape(B, S, DIFF_HEADS, 2, HEAD_DIM).transpose(0, 2, 1, 3, 4),
                         diff_q_gain[j])
            kc = rmsnorm(kc.reshape(B, S, DIFF_HEADS, 2, HEAD_DIM).transpose(0, 2, 1, 3, 4),
                         diff_k_gain[j])
            lam_init = 0.8 - 0.6 * math.exp(-0.3 * layer)
            lp = diff_lambda[j].astype(jnp.float32)
            lam = (jnp.exp(jnp.sum(lp[0] * lp[1])) - jnp.exp(jnp.sum(lp[2] * lp[3]))
                   + lam_init)
            oc = diff_attention(qc, kc, to_heads(vc, DIFF_HEADS), lam, diff_slopes)
            oc = rmsnorm(oc, diff_subln_gain[j]) * (1.0 - lam_init)
            yc = from_heads(oc)
            qd = rmsnorm(qd.reshape(B, S, SWA_Q_HEADS, HEAD_DIM), swa_q_gain[j])
            kd = rmsnorm(kd.reshape(B, S, SWA_KV_HEADS, HEAD_DIM), swa_k_gain[j])
            vd = vd.reshape(B, S, SWA_KV_HEADS, HEAD_DIM)
            yd = swa_attention(qd, kd, vd, swa_sinks[j], swa_slopes)
            y = jnp.concatenate([yc * jax.nn.silu(gc), yd * jax.nn.silu(gd)], axis=-1)
            x = x + y @ w_out_cd[j]
    return x
```

```python
import functools
import math

import numpy as np
import jax
import jax.numpy as jnp
from jax import lax
from jax.experimental import pallas as pl
from jax.experimental.pallas import tpu as pltpu

F32 = jnp.float32
BF16 = jnp.bfloat16

LANES = 128
HEAD_DIM = 64
HALF = 64
EPS = 1e-6
SCALE = HEAD_DIM ** -0.5
TM = 512
TQ = 256
TK = 256
SWA_BLOCK = 128
MOBA_BLOCK = 256
MOBA_TOPK = 3
MASK_NEG = -1e30
SEL_NEG = -1e9
VMEM_LIMIT = 56 * 1024 * 1024

F_ALIBI = 0
F_FOX = 0
F_SEL = 8
F_ONE = 0


def _alibi_slopes(n):
    s = [2.0 ** (-8.0 * (i + 1) / n) for i in range(n)]
    for v in s:
        m, _ = math.frexp(v)
        assert m == 0.5, "feature-lane ALiBi needs power-of-two slopes"
    return s


def _cparams(sem):
    return pltpu.CompilerParams(dimension_semantics=sem, vmem_limit_bytes=VMEM_LIMIT)


def _group_sums(sq, e_ref):
    outs = []
    e = e_ref[...]
    for c in range(0, sq.shape[1], 256):
        part = sq[:, c:c + 256]
        hi = part.astype(BF16)
        lo = (part - hi.astype(F32)).astype(BF16)
        outs.append(jnp.dot(hi, e, preferred_element_type=F32)
                    + jnp.dot(lo, e, preferred_element_type=F32))
    return outs[0] if len(outs) == 1 else jnp.concatenate(outs, axis=1)


def _head_norm(p, gain_row, e_ref):
    ssq = _group_sums(p * p, e_ref)
    return p * lax.rsqrt(ssq * (1.0 / HEAD_DIM) + EPS) * gain_row


def _lane_iota(rows):
    return lax.broadcasted_iota(jnp.int32, (rows, LANES), 1)


def _pos_features(i, tiles_per_seq):
    lane = _lane_iota(TM)
    pos = (i % tiles_per_seq) * TM + lax.broadcasted_iota(jnp.int32, (TM, LANES), 0)
    a64 = ((pos >> 6) << 6).astype(F32)
    a = (pos >> 6).astype(F32)
    b = (pos & 63).astype(F32)
    blk = pos >> 8
    out = []
    for j in (0, 1):
        base = (1 - j) * HALF + F_ALIBI
        qb = jnp.where(lane == base, -a64,
             jnp.where(lane == base + 1, -b,
             jnp.where(lane == base + 2, 64.0,
             jnp.where(lane == base + 3, 1.0, 0.0))))
        kf = jnp.where((lane == base) | (lane == base + 1), 1.0,
             jnp.where(lane == base + 2, a,
             jnp.where(lane == base + 3, b, 0.0)))
        sel = jnp.where(lane - ((1 - j) * HALF + F_SEL) == blk, 1.0, 0.0)
        out.append((qb, kf, sel))
    return lane, out


def _rmsnorm_rows(x, gain_row):
    return x * lax.rsqrt(jnp.mean(x * x, axis=-1, keepdims=True) + EPS) * gain_row


def _store_blocks(dst_ref, src, feats):
    lane = _lane_iota(src.shape[0])
    for p in range(src.shape[1] // LANES):
        pair = src[:, p * LANES:(p + 1) * LANES]
        for j in (0, 1):
            data = (lane >= HALF) if j == 1 else (lane < HALF)
            blk = jnp.where(data, pair, feats(p, j))
            dst_ref[:, (2 * p + j) * LANES:(2 * p + j + 1) * LANES] = blk.astype(BF16)


def _value_blocks(dst_ref, src):
    lane = _lane_iota(src.shape[0])
    for p in range(src.shape[1] // LANES):
        pair = src[:, p * LANES:(p + 1) * LANES]
        for j in (0, 1):
            data = (lane >= HALF) if j == 1 else (lane < HALF)
            one = jnp.where(lane == (1 - j) * HALF + F_ONE, 1.0, 0.0)
            dst_ref[:, (2 * p + j) * LANES:(2 * p + j + 1) * LANES] = jnp.where(data, pair, one).astype(BF16)


def _in_ab_kernel(x_ref, ng_ref, w_ref, gqa_ref, gka_ref, gqb_ref, gkb_ref, bf_ref, e_ref, tri_ref,
                  qa_ref, ka_ref, va_ref, ga_ref, qb_ref, kb_ref, vb_ref, gb_ref, km_ref,
                  carry_ref, *, tiles_per_seq):
    i = pl.program_id(0)
    h = _rmsnorm_rows(x_ref[...], ng_ref[...]).astype(BF16)

    def proj(c0, width):
        return jnp.dot(h, w_ref[:, c0:c0 + width], preferred_element_type=F32)

    lane, pf = _pos_features(i, tiles_per_seq)
    slopes = _alibi_slopes(8)

    qa = _head_norm(proj(0, 512), gqa_ref[...] * SCALE, e_ref)
    _store_blocks(qa_ref, qa, lambda p, j: pf[j][0] * slopes[2 * p + j])
    ka = _head_norm(proj(512, 512), gka_ref[...], e_ref)
    _store_blocks(ka_ref, ka, lambda p, j: pf[j][1] + pf[j][2])
    km = [jnp.mean(ka[r * MOBA_BLOCK:(r + 1) * MOBA_BLOCK], axis=0, keepdims=True)
          for r in range(TM // MOBA_BLOCK)]
    km_ref[0] = jnp.concatenate(km, axis=0)
    _value_blocks(va_ref, proj(1024, 512))
    ga_ref[...] = proj(1536, 512).astype(BF16)

    z = proj(4096, LANES) + bf_ref[...]
    logf = jnp.minimum(z, 0.0) - jnp.log1p(jnp.exp(-jnp.abs(z)))
    @pl.when(i % tiles_per_seq == 0)
    def _():
        carry_ref[...] = jnp.zeros_like(carry_ref)

    carry = carry_ref[...]
    tri = tri_ref[...]
    parts = []
    for r in range(TM // 256):
        lf = logf[r * 256:(r + 1) * 256]
        hi = lf.astype(BF16)
        r1 = lf - hi.astype(F32)
        mid = r1.astype(BF16)
        lo = (r1 - mid.astype(F32)).astype(BF16)
        cs = (jnp.dot(tri, hi, preferred_element_type=F32)
              + jnp.dot(tri, mid, preferred_element_type=F32)
              + jnp.dot(tri, lo, preferred_element_type=F32)) + carry
        carry = cs[255:256, :]
        parts.append(cs)
    carry_ref[...] = carry
    c = jnp.concatenate(parts, axis=0)
    chi = c.astype(BF16).astype(F32)
    r1 = c - chi
    cmid = r1.astype(BF16).astype(F32)
    clo = r1 - cmid

    def bcast(arr, hh):
        return jnp.broadcast_to(arr[:, hh:hh + 1], (TM, LANES))

    def fox_q(p, j):
        hh, base = 2 * p + j, (1 - j) * HALF + F_FOX
        ones = jnp.where((lane >= base + 3) & (lane < base + 6), 1.0, 0.0)
        return jnp.where(lane == base, bcast(chi, hh),
               jnp.where(lane == base + 1, bcast(cmid, hh),
               jnp.where(lane == base + 2, bcast(clo, hh), ones)))

    def fox_k(p, j):
        hh, base = 2 * p + j, (1 - j) * HALF + F_FOX
        ones = jnp.where((lane >= base) & (lane < base + 3), 1.0, 0.0)
        return jnp.where(lane == base + 3, -bcast(chi, hh),
               jnp.where(lane == base + 4, -bcast(cmid, hh),
               jnp.where(lane == base + 5, -bcast(clo, hh), ones)))

    qb = _head_norm(proj(2048, 512), gqb_ref[...] * SCALE, e_ref)
    _store_blocks(qb_ref, qb, fox_q)
    kb = _head_norm(proj(2560, 512), gkb_ref[...], e_ref)
    _store_blocks(kb_ref, kb, fox_k)
    _value_blocks(vb_ref, proj(3072, 512))
    gb_ref[...] = proj(3584, 512).astype(BF16)


def _in_ab(x2, ng, w, gqa, gka, gqb, gkb, bf, e256, tri, *, seq):
    n = x2.shape[0]
    nt = n // TM
    row = lambda width: pl.BlockSpec((TM, width), lambda i: (i, 0))
    full = lambda a: pl.BlockSpec(a.shape, lambda i: (0,) * a.ndim)
    bf16 = lambda width: jax.ShapeDtypeStruct((n, width), BF16)
    return pl.pallas_call(
        functools.partial(_in_ab_kernel, tiles_per_seq=seq // TM),
        grid=(nt,),
        in_specs=[row(x2.shape[1]), full(ng), full(w), full(gqa), full(gka), full(gqb), full(gkb),
                  full(bf), full(e256), full(tri)],
        out_specs=[row(1024), row(1024), row(1024), row(512), row(1024), row(1024), row(1024), row(512),
                   pl.BlockSpec((1, TM // MOBA_BLOCK, 512), lambda i: (i, 0, 0))],
        out_shape=[bf16(1024), bf16(1024), bf16(1024), bf16(512), bf16(1024), bf16(1024), bf16(1024),
                   bf16(512), jax.ShapeDtypeStruct((nt, TM // MOBA_BLOCK, 512), F32)],
        scratch_shapes=[pltpu.VMEM((1, LANES), F32)],
        compiler_params=_cparams(("arbitrary",)),
        name="in_proj_ab",
    )(x2, ng, w, gqa, gka, gqb, gkb, bf, e256, tri)


def _in_cd_kernel(x_ref, ng_ref, w_ref, gqc_ref, gkc_ref, gqd_ref, gkd_ref, e_ref,
                  qc_ref, kc_ref, vc_ref, gc_ref, qd_ref, kd_ref, vd_ref, gd_ref, *, tiles_per_seq):
    i = pl.program_id(0)
    h = _rmsnorm_rows(x_ref[...], ng_ref[...]).astype(BF16)

    def proj(c0, width):
        return jnp.dot(h, w_ref[:, c0:c0 + width], preferred_element_type=F32)

    _, pf = _pos_features(i, tiles_per_seq)
    dslopes = _alibi_slopes(4)
    sslopes = _alibi_slopes(8)

    qc = _head_norm(proj(0, 512), gqc_ref[...] * SCALE, e_ref)
    _store_blocks(qc_ref, qc, lambda p, j: pf[j][0] * dslopes[p])
    kc = _head_norm(proj(512, 512), gkc_ref[...], e_ref)
    _store_blocks(kc_ref, kc, lambda p, j: pf[j][1])
    vc_ref[...] = proj(1024, 512).astype(BF16)
    gc_ref[...] = proj(1536, 512).astype(BF16)

    qd = _head_norm(proj(2048, 512), gqd_ref[...] * SCALE, e_ref)
    _store_blocks(qd_ref, qd, lambda p, j: pf[j][0] * sslopes[p + 4 * j])
    kv = proj(2560, 256)
    kd = _head_norm(kv, gkd_ref[...], e_ref)[:, :LANES]
    _store_blocks(kd_ref, kd, lambda p, j: pf[j][1])
    _value_blocks(vd_ref, kv[:, LANES:])
    gd_ref[...] = proj(2816, 512).astype(BF16)


def _in_cd(x2, ng, w, gqc, gkc, gqd, gkd, e256, *, seq):
    n = x2.shape[0]
    nt = n // TM
    row = lambda width: pl.BlockSpec((TM, width), lambda i: (i, 0))
    full = lambda a: pl.BlockSpec(a.shape, lambda i: (0,) * a.ndim)
    bf16 = lambda width: jax.ShapeDtypeStruct((n, width), BF16)
    return pl.pallas_call(
        functools.partial(_in_cd_kernel, tiles_per_seq=seq // TM),
        grid=(nt,),
        in_specs=[row(x2.shape[1]), full(ng), full(w), full(gqc), full(gkc), full(gqd), full(gkd),
                  full(e256)],
        out_specs=[row(1024), row(1024), row(512), row(512), row(1024), row(256), row(256), row(512)],
        out_shape=[bf16(1024), bf16(1024), bf16(512), bf16(512), bf16(1024), bf16(256), bf16(256),
                   bf16(512)],
        compiler_params=_cparams(("arbitrary",)),
        name="in_proj_cd",
    )(x2, ng, w, gqc, gkc, gqd, gkd, e256)


def _out_kernel(x_ref, ya_ref, yb_ref, w_ref, o_ref):
    half = ya_ref.shape[1]
    o_ref[...] = (x_ref[...]
                  + jnp.dot(ya_ref[...], w_ref[:half, :], preferred_element_type=F32)
                  + jnp.dot(yb_ref[...], w_ref[half:, :], preferred_element_type=F32))


def _out_proj(x2, ya, yb, w):
    n, d = x2.shape
    row = lambda width: pl.BlockSpec((TM, width), lambda i: (i, 0))
    return pl.pallas_call(
        _out_kernel,
        grid=(n // TM,),
        in_specs=[row(d), row(ya.shape[1]), row(yb.shape[1]), pl.BlockSpec(w.shape, lambda i: (0, 0))],
        out_specs=row(d),
        out_shape=jax.ShapeDtypeStruct((n, d), F32),
        input_output_aliases={0: 0},
        compiler_params=_cparams(("arbitrary",)),
        name="out_proj",
    )(x2, ya, yb, w)


def _qk(q, k):
    return lax.dot_general(q, k, (((1,), (1,)), ((), ())), preferred_element_type=F32)


def _fold_lanes(s, op):
    out = s[:, :LANES]
    for c in range(LANES, s.shape[1], LANES):
        out = op(out, s[:, c:c + LANES])
    return out


def _causal_scores(qj, kblk, s_scr, qi):
    def body(kb, mrun):
        s = _qk(qj, kblk(kb))
        s_scr[kb] = s
        return jnp.maximum(mrun, _fold_lanes(s, jnp.maximum))

    mrun = lax.fori_loop(0, qi, body, jnp.full((TQ, LANES), MASK_NEG, F32))
    s = _qk(qj, kblk(qi))
    row = lax.broadcasted_iota(jnp.int32, (TQ, TK), 0)
    col = lax.broadcasted_iota(jnp.int32, (TQ, TK), 1)
    s = jnp.where(row >= col, s, MASK_NEG)
    s_scr[qi] = s
    mrun = jnp.maximum(mrun, _fold_lanes(s, jnp.maximum))
    return jnp.max(mrun, axis=-1, keepdims=True)


def _causal_pv(m, vblk, s_scr, qi, with_l):
    def body(kb, carry):
        p = jnp.exp(s_scr[kb] - m)
        pv = jnp.dot(p.astype(BF16), vblk(kb), preferred_element_type=F32)
        if with_l:
            return carry[0] + pv, carry[1] + _fold_lanes(p, jnp.add)
        return (carry[0] + pv,)

    init = (jnp.zeros((TQ, LANES), F32),) * (2 if with_l else 1)
    return lax.fori_loop(0, qi + 1, body, init)


def _rows(ref, kb, c0):
    return ref[0, pl.ds(pl.multiple_of(kb * TK, TK), TK), c0:c0 + LANES]


def _silu(g):
    return g * (1.0 / (1.0 + jnp.exp(-g)))


def _moba_select(qj, km_ref, j, qi):
    lane8 = lax.broadcasted_iota(jnp.int32, (8, LANES), 1)
    data8 = (lane8 >= HALF) if j == 1 else (lane8 < HALF)
    km = jnp.where(data8, km_ref[0], 0.0)
    hi = km.astype(BF16)
    lo = (km - hi.astype(F32)).astype(BF16)
    g2x = _qk(jnp.concatenate([hi, lo], axis=0), qj)
    gate = g2x[:8] + g2x[8:]
    n_iota = lax.broadcasted_iota(jnp.int32, (8, TQ), 0)
    cnt = jnp.zeros((8, TQ), F32)
    for n2 in range(8):
        g2 = jnp.broadcast_to(gate[n2:n2 + 1, :], (8, TQ))
        beats = (g2 > gate) | ((g2 == gate) & (n_iota > n2))
        cnt = cnt + jnp.where(beats & (qi > n2), 1.0, 0.0)
    keep = ((n_iota < qi) & (cnt < MOBA_TOPK)) | (n_iota == qi)
    selb = jnp.where(keep, 0.0, SEL_NEG)
    base = (1 - j) * HALF + F_SEL
    planes = jnp.concatenate([jnp.zeros((base, TQ), F32), selb,
                              jnp.zeros((LANES - base - 8, TQ), F32)], axis=0)
    feat = planes.T
    lane = _lane_iota(TQ)
    return jnp.where((lane >= base) & (lane < base + 8), feat.astype(BF16), qj)


def _attn_ab_kernel(*refs, moba):
    if moba:
        q_ref, k_ref, v_ref, g_ref, km_ref, o_ref, s_scr = refs
    else:
        q_ref, k_ref, v_ref, g_ref, o_ref, s_scr = refs
    qi = pl.program_id(2)
    lane = _lane_iota(TQ)
    outs = []
    for j in (0, 1):
        c0 = j * LANES
        qj = q_ref[0, :, c0:c0 + LANES]
        if moba:
            qj = _moba_select(qj, km_ref, j, qi)
        m = _causal_scores(qj, lambda kb: _rows(k_ref, kb, c0), s_scr, qi)
        (acc,) = _causal_pv(m, lambda kb: _rows(v_ref, kb, c0), s_scr, qi, False)
        lcol = (1 - j) * HALF + F_ONE
        l = jnp.broadcast_to(acc[:, lcol:lcol + 1], (TQ, LANES))
        outs.append(acc / l)
    o = jnp.where(lane < HALF, outs[0], outs[1])
    o_ref[0] = (o * _silu(g_ref[0].astype(F32))).astype(BF16)


def _attn_ab(q, k, v, g, km, *, moba):
    b, s, _ = q.shape
    nq = s // TQ
    in_specs = [pl.BlockSpec((1, TQ, 2 * LANES), lambda bi, p, qi: (bi, qi, p)),
                pl.BlockSpec((1, s, 2 * LANES), lambda bi, p, qi: (bi, 0, p)),
                pl.BlockSpec((1, s, 2 * LANES), lambda bi, p, qi: (bi, 0, p)),
                pl.BlockSpec((1, TQ, LANES), lambda bi, p, qi: (bi, qi, p))]
    args = [q, k, v, g]
    if moba:
        in_specs.append(pl.BlockSpec((1, 8, LANES), lambda bi, p, qi: (bi, 0, p)))
        args.append(km)
    return pl.pallas_call(
        functools.partial(_attn_ab_kernel, moba=moba),
        grid=(b, 4, nq),
        in_specs=in_specs,
        out_specs=pl.BlockSpec((1, TQ, LANES), lambda bi, p, qi: (bi, qi, p)),
        out_shape=jax.ShapeDtypeStruct((b, s, 4 * LANES), BF16),
        scratch_shapes=[pltpu.VMEM((s // TK, TQ, TK), F32)],
        compiler_params=_cparams(("arbitrary", "arbitrary", "arbitrary")),
        name="attn_moba" if moba else "attn_fox",
    )(*args)


def _attn_diff_kernel(q_ref, k_ref, v_ref, g_ref, lam_ref, sg_ref, o_ref, s_scr, *, lam_init):
    qi = pl.program_id(2)
    lp = lam_ref[...]
    s1 = jnp.sum(lp[0:1] * lp[1:2], axis=-1, keepdims=True)
    s2 = jnp.sum(lp[2:3] * lp[3:4], axis=-1, keepdims=True)
    lam = jnp.exp(s1) - jnp.exp(s2) + lam_init
    outs = []
    for j in (0, 1):
        c0 = j * LANES
        qj = q_ref[0, :, c0:c0 + LANES]
        m = _causal_scores(qj, lambda kb: _rows(k_ref, kb, c0), s_scr, qi)
        acc, lrun = _causal_pv(m, lambda kb: _rows(v_ref, kb, 0), s_scr, qi, True)
        outs.append(acc / jnp.sum(lrun, axis=-1, keepdims=True))
    oc = outs[0] - lam * outs[1]
    oc = _rmsnorm_rows(oc, sg_ref[...]) * (1.0 - lam_init)
    o_ref[0] = (oc * _silu(g_ref[0].astype(F32))).astype(BF16)


def _attn_diff(q, k, v, g, lam, sg, *, lam_init):
    b, s, _ = q.shape
    nq = s // TQ
    return pl.pallas_call(
        functools.partial(_attn_diff_kernel, lam_init=lam_init),
        grid=(b, 4, nq),
        in_specs=[pl.BlockSpec((1, TQ, 2 * LANES), lambda bi, p, qi: (bi, qi, p)),
                  pl.BlockSpec((1, s, 2 * LANES), lambda bi, p, qi: (bi, 0, p)),
                  pl.BlockSpec((1, s, LANES), lambda bi, p, qi: (bi, 0, p)),
                  pl.BlockSpec((1, TQ, LANES), lambda bi, p, qi: (bi, qi, p)),
                  pl.BlockSpec(lam.shape, lambda bi, p, qi: (0, 0)),
                  pl.BlockSpec(sg.shape, lambda bi, p, qi: (0, 0))],
        out_specs=pl.BlockSpec((1, TQ, LANES), lambda bi, p, qi: (bi, qi, p)),
        out_shape=jax.ShapeDtypeStruct((b, s, 4 * LANES), BF16),
        scratch_shapes=[pltpu.VMEM((s // TK, TQ, TK), F32)],
        compiler_params=_cparams(("arbitrary", "arbitrary", "arbitrary")),
        name="attn_diff",
    )(q, k, v, g, lam, sg)


def _attn_swa_kernel(q_ref, k_ref, v_ref, g_ref, sink_ref, o_ref, acc_scr):
    qi = pl.program_id(2)
    t0 = qi * TQ
    w = SWA_BLOCK
    lane = _lane_iota(w)

    def sub_block(r0, key0, nkeys, j):
        c0 = j * LANES
        qj = q_ref[0, r0:r0 + w, c0:c0 + LANES]
        if not isinstance(key0, int):
            key0 = pl.multiple_of(key0, w)
        kk = k_ref[0, pl.ds(key0, nkeys), c0:c0 + LANES]
        vv = v_ref[0, pl.ds(key0, nkeys), c0:c0 + LANES]
        s = _qk(qj, kk)
        row = lax.broadcasted_iota(jnp.int32, (w, nkeys), 0)
        col = lax.broadcasted_iota(jnp.int32, (w, nkeys), 1)
        rel = row + (nkeys - w) - col
        s = jnp.where((rel >= 0) & (rel < w), s, MASK_NEG)
        sink = sink_ref[0, j:j + 1, :]
        m = jnp.maximum(jnp.max(s, axis=-1, keepdims=True), sink[:, 0:1])
        p = jnp.exp(s - m)
        acc = jnp.dot(p.astype(BF16), vv, preferred_element_type=F32)
        lcol = (1 - j) * HALF + F_ONE
        l = jnp.broadcast_to(acc[:, lcol:lcol + 1], (w, LANES)) + jnp.exp(sink - m)
        if j == 0:
            acc_scr[r0:r0 + w, :] = acc / l
        else:
            acc_scr[r0:r0 + w, :] = jnp.where(lane >= HALF, acc / l, acc_scr[r0:r0 + w, :])

    for j in (0, 1):
        @pl.when(qi == 0)
        def _():
            sub_block(0, 0, w, j)

        @pl.when(qi > 0)
        def _():
            sub_block(0, t0 - w, 2 * w, j)

        sub_block(w, t0, 2 * w, j)
    o_ref[0] = (acc_scr[...] * _silu(g_ref[0].astype(F32))).astype(BF16)


def _attn_swa(q, k, v, g, sinks):
    b, s, _ = q.shape
    nq = s // TQ
    return pl.pallas_call(
        _attn_swa_kernel,
        grid=(b, 4, nq),
        in_specs=[pl.BlockSpec((1, TQ, 2 * LANES), lambda bi, p, qi: (bi, qi, p)),
                  pl.BlockSpec((1, s, 2 * LANES), lambda bi, p, qi: (bi, 0, 0)),
                  pl.BlockSpec((1, s, 2 * LANES), lambda bi, p, qi: (bi, 0, 0)),
                  pl.BlockSpec((1, TQ, LANES), lambda bi, p, qi: (bi, qi, p)),
                  pl.BlockSpec((1, 2, LANES), lambda bi, p, qi: (p, 0, 0))],
        out_specs=pl.BlockSpec((1, TQ, LANES), lambda bi, p, qi: (bi, qi, p)),
        out_shape=jax.ShapeDtypeStruct((b, s, 4 * LANES), BF16),
        scratch_shapes=[pltpu.VMEM((TQ, LANES), F32)],
        compiler_params=_cparams(("arbitrary", "arbitrary", "arbitrary")),
        name="attn_swa",
    )(q, k, v, g, sinks)


SWA_HEAD_ORDER = (0, 4, 1, 5, 2, 6, 3, 7)


def _tile_gain(g, reps):
    return jnp.tile(g.astype(F32), reps)[None, :]


def kernel(x, norm_gain, w_in_ab, b_forget, moba_q_gain, moba_k_gain, fox_q_gain, fox_k_gain, w_out_ab,
           w_in_cd, diff_q_gain, diff_k_gain, diff_lambda, diff_subln_gain, swa_q_gain, swa_k_gain,
           swa_sinks, w_out_cd):
    b, s, d = x.shape
    assert d == 1024 and s % TM == 0 and s // MOBA_BLOCK == 8
    depth = norm_gain.shape[0]
    x2 = x.reshape(b * s, d)

    grp = np.arange(256) // HEAD_DIM
    e256 = jnp.asarray(grp[:, None] == grp[None, :], dtype=BF16)
    tri = jnp.asarray(np.tril(np.ones((256, 256), np.float32)), dtype=BF16)
    perm = np.concatenate([np.arange(HEAD_DIM) + HEAD_DIM * hh for hh in SWA_HEAD_ORDER])

    for layer in range(depth):
        j = layer // 2
        ng = norm_gain[layer][None, :]
        if layer % 2 == 0:
            w = jnp.pad(w_in_ab[j], ((0, 0), (0, 4096 + LANES - w_in_ab.shape[2]))).astype(BF16)
            bf = jnp.pad(b_forget[j].astype(F32), (0, LANES - b_forget.shape[1]))[None, :]
            qa, ka, va, ga, qb, kb, vb, gb, km = _in_ab(
                x2, ng, w, _tile_gain(moba_q_gain[j], 8), _tile_gain(moba_k_gain[j], 8),
                _tile_gain(fox_q_gain[j], 8), _tile_gain(fox_k_gain[j], 8), bf, e256, tri, seq=s)
            r3 = lambda a: a.reshape(b, s, a.shape[1])
            ya = _attn_ab(r3(qa), r3(ka), r3(va), r3(ga), km.reshape(b, 8, 512), moba=True)
            yb = _attn_ab(r3(qb), r3(kb), r3(vb), r3(gb), None, moba=False)
            x2 = _out_proj(x2, ya.reshape(b * s, 512), yb.reshape(b * s, 512), w_out_ab[j].astype(BF16))
        else:
            wj = w_in_cd[j]
            w = jnp.concatenate([wj[:, :2048], wj[:, 2048:2560][:, perm], wj[:, 2560:2816],
                                 wj[:, 2816:3328][:, perm]], axis=1).astype(BF16)
            qc, kc, vc, gc, qd, kd, vd, gd = _in_cd(
                x2, ng, w, _tile_gain(diff_q_gain[j], 8), _tile_gain(diff_k_gain[j], 8),
                _tile_gain(swa_q_gain[j], 8), _tile_gain(swa_k_gain[j], 4), e256, seq=s)
            r3 = lambda a: a.reshape(b, s, a.shape[1])
            lam_init = 0.8 - 0.6 * math.exp(-0.3 * layer)
            yc = _attn_diff(r3(qc), r3(kc), r3(vc), r3(gc), diff_lambda[j].astype(F32),
                            diff_subln_gain[j].astype(F32)[None, :], lam_init=lam_init)
            sinks = swa_sinks[j].astype(F32)[np.asarray(SWA_HEAD_ORDER)].reshape(4, 2, 1)
            yd = _attn_swa(r3(qd), r3(kd), r3(vd), r3(gd), jnp.broadcast_to(sinks, (4, 2, LANES)))
            wo = w_out_cd[j]
            wo = jnp.concatenate([wo[:512], wo[512:][perm]], axis=0).astype(BF16)
            x2 = _out_proj(x2, yc.reshape(b * s, 512), yd.reshape(b * s, 512), wo)
    return x2.reshape(b, s, d)
```

```python
import functools
import math

import numpy as np
import jax
import jax.numpy as jnp
from jax import lax
from jax.experimental import pallas as pl
from jax.experimental.pallas import tpu as pltpu

F32 = jnp.float32
BF16 = jnp.bfloat16

LANES = 128
HEAD_DIM = 64
HALF = 64
EPS = 1e-6
SCALE = HEAD_DIM ** -0.5
TM = 512
TQ = 256
TK = 256
SWA_BLOCK = 128
MOBA_BLOCK = 256
MOBA_TOPK = 3
MASK_NEG = -1e30
SEL_NEG = -1e9
VMEM_LIMIT = 56 * 1024 * 1024

F_ALIBI = 0
F_FOX = 0
F_SEL = 8
F_ONE = 0


def _alibi_slopes(n):
    s = [2.0 ** (-8.0 * (i + 1) / n) for i in range(n)]
    for v in s:
        m, _ = math.frexp(v)
        assert m == 0.5, "feature-lane ALiBi needs power-of-two slopes"
    return s


def _cparams(sem):
    return pltpu.CompilerParams(dimension_semantics=sem, vmem_limit_bytes=VMEM_LIMIT)


def _group_sums(sq, e_ref):
    outs = []
    e = e_ref[...]
    for c in range(0, sq.shape[1], 256):
        part = sq[:, c:c + 256]
        hi = part.astype(BF16)
        lo = (part - hi.astype(F32)).astype(BF16)
        outs.append(jnp.dot(hi, e, preferred_element_type=F32)
                    + jnp.dot(lo, e, preferred_element_type=F32))
    return outs[0] if len(outs) == 1 else jnp.concatenate(outs, axis=1)


def _head_norm(p, gain_row, e_ref):
    ssq = _group_sums(p * p, e_ref)
    return p * lax.rsqrt(ssq * (1.0 / HEAD_DIM) + EPS) * gain_row


def _lane_iota(rows):
    return lax.broadcasted_iota(jnp.int32, (rows, LANES), 1)


def _pos_features(i, tiles_per_seq):
    lane = _lane_iota(TM)
    pos = (i % tiles_per_seq) * TM + lax.broadcasted_iota(jnp.int32, (TM, LANES), 0)
    a64 = ((pos >> 6) << 6).astype(F32)
    a = (pos >> 6).astype(F32)
    b = (pos & 63).astype(F32)
    blk = pos >> 8
    out = []
    for j in (0, 1):
        base = (1 - j) * HALF + F_ALIBI
        qb = jnp.where(lane == base, -a64,
             jnp.where(lane == base + 1, -b,
             jnp.where(lane == base + 2, 64.0,
             jnp.where(lane == base + 3, 1.0, 0.0))))
        kf = jnp.where((lane == base) | (lane == base + 1), 1.0,
             jnp.where(lane == base + 2, a,
             jnp.where(lane == base + 3, b, 0.0)))
        sel = jnp.where(lane - ((1 - j) * HALF + F_SEL) == blk, 1.0, 0.0)
        out.append((qb, kf, sel))
    return lane, out


def _rmsnorm_rows(x, gain_row):
    return x * lax.rsqrt(jnp.mean(x * x, axis=-1, keepdims=True) + EPS) * gain_row


def _store_blocks(dst_ref, src, feats):
    lane = _lane_iota(src.shape[0])
    for p in range(src.shape[1] // LANES):
        pair = src[:, p * LANES:(p + 1) * LANES]
        for j in (0, 1):
            data = (lane >= HALF) if j == 1 else (lane < HALF)
            blk = jnp.where(data, pair, feats(p, j))
            dst_ref[:, (2 * p + j) * LANES:(2 * p + j + 1) * LANES] = blk.astype(BF16)


def _value_blocks(dst_ref, src):
    lane = _lane_iota(src.shape[0])
    for p in range(src.shape[1] // LANES):
        pair = src[:, p * LANES:(p + 1) * LANES]
        for j in (0, 1):
            data = (lane >= HALF) if j == 1 else (lane < HALF)
            one = jnp.where(lane == (1 - j) * HALF + F_ONE, 1.0, 0.0)
            dst_ref[:, (2 * p + j) * LANES:(2 * p + j + 1) * LANES] = jnp.where(data, pair, one).astype(BF16)


def _wide_value_blocks(dst_ref, src):
    lane = _lane_iota(src.shape[0])
    one = jnp.where(lane == F_ONE, 1.0, 0.0).astype(BF16)
    for p in range(src.shape[1] // LANES):
        dst_ref[:, 2 * p * LANES:(2 * p + 1) * LANES] = src[:, p * LANES:(p + 1) * LANES].astype(BF16)
        dst_ref[:, (2 * p + 1) * LANES:(2 * p + 2) * LANES] = one


def _in_ab_kernel(x_ref, ng_ref, w_ref, gqa_ref, gka_ref, gqb_ref, gkb_ref, bf_ref, e_ref, tri_ref,
                  qa_ref, ka_ref, va_ref, ga_ref, qb_ref, kb_ref, vb_ref, gb_ref, km_ref,
                  carry_ref, *, tiles_per_seq):
    i = pl.program_id(0)
    h = _rmsnorm_rows(x_ref[...], ng_ref[...]).astype(BF16)

    def proj(c0, width):
        return jnp.dot(h, w_ref[:, c0:c0 + width], preferred_element_type=F32)

    lane, pf = _pos_features(i, tiles_per_seq)
    slopes = _alibi_slopes(8)

    qa = _head_norm(proj(0, 512), gqa_ref[...] * SCALE, e_ref)
    _store_blocks(qa_ref, qa, lambda p, j: pf[j][0] * slopes[2 * p + j])
    ka = _head_norm(proj(512, 512), gka_ref[...], e_ref)
    _store_blocks(ka_ref, ka, lambda p, j: pf[j][1] + pf[j][2])
    km = [jnp.mean(ka[r * MOBA_BLOCK:(r + 1) * MOBA_BLOCK], axis=0, keepdims=True)
          for r in range(TM // MOBA_BLOCK)]
    km_ref[0] = jnp.concatenate(km, axis=0)
    _value_blocks(va_ref, proj(1024, 512))
    ga_ref[...] = proj(1536, 512).astype(BF16)

    z = proj(4096, LANES) + bf_ref[...]
    logf = jnp.minimum(z, 0.0) - jnp.log1p(jnp.exp(-jnp.abs(z)))

    @pl.when(i % tiles_per_seq == 0)
    def _():
        carry_ref[...] = jnp.zeros_like(carry_ref)

    carry = carry_ref[...]
    tri = tri_ref[...]
    parts = []
    for r in range(TM // 256):
        lf = logf[r * 256:(r + 1) * 256]
        hi = lf.astype(BF16)
        r1 = lf - hi.astype(F32)
        mid = r1.astype(BF16)
        lo = (r1 - mid.astype(F32)).astype(BF16)
        cs = (jnp.dot(tri, hi, preferred_element_type=F32)
              + jnp.dot(tri, mid, preferred_element_type=F32)
              + jnp.dot(tri, lo, preferred_element_type=F32)) + carry
        carry = cs[255:256, :]
        parts.append(cs)
    carry_ref[...] = carry
    c = jnp.concatenate(parts, axis=0)
    chi = c.astype(BF16).astype(F32)
    r1 = c - chi
    cmid = r1.astype(BF16).astype(F32)
    clo = r1 - cmid

    def bcast(arr, hh):
        return jnp.broadcast_to(arr[:, hh:hh + 1], (TM, LANES))

    def fox_q(p, j):
        hh, base = 2 * p + j, (1 - j) * HALF + F_FOX
        ones = jnp.where((lane >= base + 3) & (lane < base + 6), 1.0, 0.0)
        return jnp.where(lane == base, bcast(chi, hh),
               jnp.where(lane == base + 1, bcast(cmid, hh),
               jnp.where(lane == base + 2, bcast(clo, hh), ones)))

    def fox_k(p, j):
        hh, base = 2 * p + j, (1 - j) * HALF + F_FOX
        ones = jnp.where((lane >= base) & (lane < base + 3), 1.0, 0.0)
        return jnp.where(lane == base + 3, -bcast(chi, hh),
               jnp.where(lane == base + 4, -bcast(cmid, hh),
               jnp.where(lane == base + 5, -bcast(clo, hh), ones)))

    qb = _head_norm(proj(2048, 512), gqb_ref[...] * SCALE, e_ref)
    _store_blocks(qb_ref, qb, fox_q)
    kb = _head_norm(proj(2560, 512), gkb_ref[...], e_ref)
    _store_blocks(kb_ref, kb, fox_k)
    _value_blocks(vb_ref, proj(3072, 512))
    gb_ref[...] = proj(3584, 512).astype(BF16)


def _in_ab(x2, ng, w, gqa, gka, gqb, gkb, bf, e256, tri, *, seq):
    n = x2.shape[0]
    nt = n // TM
    row = lambda width: pl.BlockSpec((TM, width), lambda i: (i, 0))
    full = lambda a: pl.BlockSpec(a.shape, lambda i: (0,) * a.ndim)
    bf16 = lambda width: jax.ShapeDtypeStruct((n, width), BF16)
    return pl.pallas_call(
        functools.partial(_in_ab_kernel, tiles_per_seq=seq // TM),
        grid=(nt,),
        in_specs=[row(x2.shape[1]), full(ng), full(w), full(gqa), full(gka), full(gqb), full(gkb),
                  full(bf), full(e256), full(tri)],
        out_specs=[row(1024), row(1024), row(1024), row(512), row(1024), row(1024), row(1024), row(512),
                   pl.BlockSpec((1, TM // MOBA_BLOCK, 512), lambda i: (i, 0, 0))],
        out_shape=[bf16(1024), bf16(1024), bf16(1024), bf16(512), bf16(1024), bf16(1024), bf16(1024),
                   bf16(512), jax.ShapeDtypeStruct((nt, TM // MOBA_BLOCK, 512), F32)],
        scratch_shapes=[pltpu.VMEM((1, LANES), F32)],
        compiler_params=_cparams(("arbitrary",)),
        name="in_proj_ab",
    )(x2, ng, w, gqa, gka, gqb, gkb, bf, e256, tri)


def _in_cd_kernel(x_ref, ng_ref, w_ref, gqc_ref, gkc_ref, gqd_ref, gkd_ref, e_ref,
                  qc_ref, kc_ref, vc_ref, gc_ref, qd_ref, kd_ref, vd_ref, gd_ref, *, tiles_per_seq):
    i = pl.program_id(0)
    h = _rmsnorm_rows(x_ref[...], ng_ref[...]).astype(BF16)

    def proj(c0, width):
        return jnp.dot(h, w_ref[:, c0:c0 + width], preferred_element_type=F32)

    _, pf = _pos_features(i, tiles_per_seq)
    dslopes = _alibi_slopes(4)
    sslopes = _alibi_slopes(8)

    qc = _head_norm(proj(0, 512), gqc_ref[...] * SCALE, e_ref)
    _store_blocks(qc_ref, qc, lambda p, j: pf[j][0] * dslopes[p])
    kc = _head_norm(proj(512, 512), gkc_ref[...], e_ref)
    _store_blocks(kc_ref, kc, lambda p, j: pf[j][1])
    _wide_value_blocks(vc_ref, proj(1024, 512))
    gc_ref[...] = proj(1536, 512).astype(BF16)

    qd = _head_norm(proj(2048, 512), gqd_ref[...] * SCALE, e_ref)
    _store_blocks(qd_ref, qd, lambda p, j: pf[j][0] * sslopes[p + 4 * j])
    kv = proj(2560, 256)
    kd = _head_norm(kv, gkd_ref[...], e_ref)[:, :LANES]
    _store_blocks(kd_ref, kd, lambda p, j: pf[j][1])
    _value_blocks(vd_ref, kv[:, LANES:])
    gd_ref[...] = proj(2816, 512).astype(BF16)


def _in_cd(x2, ng, w, gqc, gkc, gqd, gkd, e256, *, seq):
    n = x2.shape[0]
    nt = n // TM
    row = lambda width: pl.BlockSpec((TM, width), lambda i: (i, 0))
    full = lambda a: pl.BlockSpec(a.shape, lambda i: (0,) * a.ndim)
    bf16 = lambda width: jax.ShapeDtypeStruct((n, width), BF16)
    return pl.pallas_call(
        functools.partial(_in_cd_kernel, tiles_per_seq=seq // TM),
        grid=(nt,),
        in_specs=[row(x2.shape[1]), full(ng), full(w), full(gqc), full(gkc), full(gqd), full(gkd),
                  full(e256)],
        out_specs=[row(1024), row(1024), row(1024), row(512), row(1024), row(256), row(256), row(512)],
        out_shape=[bf16(1024), bf16(1024), bf16(1024), bf16(512), bf16(1024), bf16(256), bf16(256),
                   bf16(512)],
        compiler_params=_cparams(("arbitrary",)),
        name="in_proj_cd",
    )(x2, ng, w, gqc, gkc, gqd, gkd, e256)


def _out_kernel(x_ref, ya_ref, yb_ref, w_ref, o_ref):
    half = ya_ref.shape[1]
    o_ref[...] = (x_ref[...]
                  + jnp.dot(ya_ref[...], w_ref[:half, :], preferred_element_type=F32)
                  + jnp.dot(yb_ref[...], w_ref[half:, :], preferred_element_type=F32))


def _out_proj(x2, ya, yb, w):
    n, d = x2.shape
    row = lambda width: pl.BlockSpec((TM, width), lambda i: (i, 0))
    return pl.pallas_call(
        _out_kernel,
        grid=(n // TM,),
        in_specs=[row(d), row(ya.shape[1]), row(yb.shape[1]), pl.BlockSpec(w.shape, lambda i: (0, 0))],
        out_specs=row(d),
        out_shape=jax.ShapeDtypeStruct((n, d), F32),
        input_output_aliases={0: 0},
        compiler_params=_cparams(("arbitrary",)),
        name="out_proj",
    )(x2, ya, yb, w)


def _qk(q, k):
    return lax.dot_general(q, k, (((1,), (1,)), ((), ())), preferred_element_type=F32)


def _silu(g):
    return g * (1.0 / (1.0 + jnp.exp(-g)))


def _causal_mask():
    row = lax.broadcasted_iota(jnp.int32, (TQ, TK), 0)
    col = lax.broadcasted_iota(jnp.int32, (TQ, TK), 1)
    return row >= col


def _causal_tile(qj, k_ref, v_ref, kc0, vc0, vw, qi, causal):
    n = (qi + 1) * TK
    s = _qk(qj, k_ref[0, :n, kc0:kc0 + LANES])
    diag = jnp.where(causal, s[:, n - TK:], MASK_NEG)
    s = diag if qi == 0 else jnp.concatenate([s[:, :n - TK], diag], axis=1)
    m = jnp.max(s, axis=-1, keepdims=True)
    p = jnp.exp(s - m).astype(BF16)
    return jnp.dot(p, v_ref[0, :n, vc0:vc0 + vw], preferred_element_type=F32)


def _moba_select(qj, km16, j, qi):
    if qi == 0:
        return qj
    g2x = _qk(km16, qj)
    gate = g2x[:8] + g2x[8:]
    n_iota = lax.broadcasted_iota(jnp.int32, (8, TQ), 0)
    cnt = jnp.zeros((8, TQ), F32)
    for n2 in range(qi):
        g2 = jnp.broadcast_to(gate[n2:n2 + 1, :], (8, TQ))
        beats = (g2 > gate) | ((g2 == gate) & (n_iota > n2))
        cnt = cnt + jnp.where(beats, 1.0, 0.0)
    keep = (cnt < MOBA_TOPK) | (n_iota >= qi)
    selb = jnp.where(keep, 0.0, SEL_NEG)
    base = (1 - j) * HALF + F_SEL
    planes = jnp.concatenate([jnp.zeros((base, TQ), F32), selb,
                              jnp.zeros((LANES - base - 8, TQ), F32)], axis=0)
    feat = planes.T
    lane = _lane_iota(TQ)
    return jnp.where((lane >= base) & (lane < base + 8), feat.astype(BF16), qj)


def _attn_ab_kernel(*refs, moba, nq):
    if moba:
        q_ref, k_ref, v_ref, g_ref, km_ref, o_ref = refs
    else:
        q_ref, k_ref, v_ref, g_ref, o_ref = refs
    lane = _lane_iota(TQ)
    causal = _causal_mask()
    km16 = []
    if moba:
        lane8 = lax.broadcasted_iota(jnp.int32, (8, LANES), 1)
        for j in (0, 1):
            km = jnp.where((lane8 >= HALF) if j == 1 else (lane8 < HALF), km_ref[0], 0.0)
            hi = km.astype(BF16)
            lo = (km - hi.astype(F32)).astype(BF16)
            km16.append(jnp.concatenate([hi, lo], axis=0))
    for qi in range(nq):
        r0 = qi * TQ
        outs = []
        for j in (0, 1):
            c0 = j * LANES
            qj = q_ref[0, r0:r0 + TQ, c0:c0 + LANES]
            if moba:
                qj = _moba_select(qj, km16[j], j, qi)
            acc = _causal_tile(qj, k_ref, v_ref, c0, c0, LANES, qi, causal)
            lcol = (1 - j) * HALF + F_ONE
            outs.append(acc / jnp.broadcast_to(acc[:, lcol:lcol + 1], (TQ, LANES)))
        o = jnp.where(lane < HALF, outs[0], outs[1])
        o_ref[0, r0:r0 + TQ, :] = (o * _silu(g_ref[0, r0:r0 + TQ, :].astype(F32))).astype(BF16)


def _pair_spec(s, width):
    return pl.BlockSpec((1, s, width), lambda bi, p: (bi, 0, p))


def _attn_ab(q, k, v, g, km, *, moba):
    b, s, _ = q.shape
    in_specs = [_pair_spec(s, 2 * LANES), _pair_spec(s, 2 * LANES), _pair_spec(s, 2 * LANES),
                _pair_spec(s, LANES)]
    args = [q, k, v, g]
    if moba:
        in_specs.append(pl.BlockSpec((1, 8, LANES), lambda bi, p: (bi, 0, p)))
        args.append(km)
    return pl.pallas_call(
        functools.partial(_attn_ab_kernel, moba=moba, nq=s // TQ),
        grid=(b, 4),
        in_specs=in_specs,
        out_specs=_pair_spec(s, LANES),
        out_shape=jax.ShapeDtypeStruct((b, s, 4 * LANES), BF16),
        compiler_params=_cparams(("arbitrary", "arbitrary")),
        name="attn_moba" if moba else "attn_fox",
    )(*args)


def _attn_diff_kernel(q_ref, k_ref, v_ref, g_ref, lam_ref, sg_ref, o_ref, *, lam_init, nq):
    lp = lam_ref[...]
    s1 = jnp.sum(lp[0:1] * lp[1:2], axis=-1, keepdims=True)
    s2 = jnp.sum(lp[2:3] * lp[3:4], axis=-1, keepdims=True)
    lam = jnp.exp(s1) - jnp.exp(s2) + lam_init
    causal = _causal_mask()
    for qi in range(nq):
        r0 = qi * TQ
        outs = []
        for j in (0, 1):
            c0 = j * LANES
            qj = q_ref[0, r0:r0 + TQ, c0:c0 + LANES]
            acc = _causal_tile(qj, k_ref, v_ref, c0, 0, 2 * LANES, qi, causal)
            outs.append(acc[:, :LANES] / jnp.broadcast_to(acc[:, LANES:LANES + 1], (TQ, LANES)))
        oc = outs[0] - lam * outs[1]
        oc = _rmsnorm_rows(oc, sg_ref[...]) * (1.0 - lam_init)
        o_ref[0, r0:r0 + TQ, :] = (oc * _silu(g_ref[0, r0:r0 + TQ, :].astype(F32))).astype(BF16)


def _attn_diff(q, k, v, g, lam, sg, *, lam_init):
    b, s, _ = q.shape
    return pl.pallas_call(
        functools.partial(_attn_diff_kernel, lam_init=lam_init, nq=s // TQ),
        grid=(b, 4),
        in_specs=[_pair_spec(s, 2 * LANES), _pair_spec(s, 2 * LANES), _pair_spec(s, 2 * LANES),
                  _pair_spec(s, LANES),
                  pl.BlockSpec(lam.shape, lambda bi, p: (0, 0)),
                  pl.BlockSpec(sg.shape, lambda bi, p: (0, 0))],
        out_specs=_pair_spec(s, LANES),
        out_shape=jax.ShapeDtypeStruct((b, s, 4 * LANES), BF16),
        compiler_params=_cparams(("arbitrary", "arbitrary")),
        name="attn_diff",
    )(q, k, v, g, lam, sg)


def _attn_swa_kernel(q_ref, k_ref, v_ref, g_ref, sink_ref, o_ref, *, nq):
    lane = _lane_iota(TQ)
    masks = {}
    for qi in range(nq):
        t0 = qi * TQ
        start = max(t0 - SWA_BLOCK, 0)
        nkeys = t0 + TQ - start
        if (nkeys, t0 - start) not in masks:
            row = lax.broadcasted_iota(jnp.int32, (TQ, nkeys), 0)
            col = lax.broadcasted_iota(jnp.int32, (TQ, nkeys), 1)
            rel = row - col + (t0 - start)
            masks[(nkeys, t0 - start)] = (rel >= 0) & (rel < SWA_BLOCK)
        ok = masks[(nkeys, t0 - start)]
        outs = []
        for j in (0, 1):
            c0 = j * LANES
            qj = q_ref[0, t0:t0 + TQ, c0:c0 + LANES]
            s = jnp.where(ok, _qk(qj, k_ref[0, start:start + nkeys, c0:c0 + LANES]), MASK_NEG)
            sink = sink_ref[0, j:j + 1, :]
            m = jnp.maximum(jnp.max(s, axis=-1, keepdims=True), sink[:, 0:1])
            p = jnp.exp(s - m).astype(BF16)
            acc = jnp.dot(p, v_ref[0, start:start + nkeys, c0:c0 + LANES], preferred_element_type=F32)
            lcol = (1 - j) * HALF + F_ONE
            l = jnp.broadcast_to(acc[:, lcol:lcol + 1], (TQ, LANES)) + jnp.exp(sink - m)
            outs.append(acc / l)
        o = jnp.where(lane < HALF, outs[0], outs[1])
        o_ref[0, t0:t0 + TQ, :] = (o * _silu(g_ref[0, t0:t0 + TQ, :].astype(F32))).astype(BF16)


def _attn_swa(q, k, v, g, sinks):
    b, s, _ = q.shape
    kv_spec = pl.BlockSpec((1, s, 2 * LANES), lambda bi, p: (bi, 0, 0))
    return pl.pallas_call(
        functools.partial(_attn_swa_kernel, nq=s // TQ),
        grid=(b, 4),
        in_specs=[_pair_spec(s, 2 * LANES), kv_spec, kv_spec, _pair_spec(s, LANES),
                  pl.BlockSpec((1, 2, LANES), lambda bi, p: (p, 0, 0))],
        out_specs=_pair_spec(s, LANES),
        out_shape=jax.ShapeDtypeStruct((b, s, 4 * LANES), BF16),
        compiler_params=_cparams(("arbitrary", "arbitrary")),
        name="attn_swa",
    )(q, k, v, g, sinks)


SWA_HEAD_ORDER = (0, 4, 1, 5, 2, 6, 3, 7)


def _tile_gain(g, reps):
    return jnp.tile(g.astype(F32), reps)[None, :]


def kernel(x, norm_gain, w_in_ab, b_forget, moba_q_gain, moba_k_gain, fox_q_gain, fox_k_gain, w_out_ab,
           w_in_cd, diff_q_gain, diff_k_gain, diff_lambda, diff_subln_gain, swa_q_gain, swa_k_gain,
           swa_sinks, w_out_cd):
    b, s, d = x.shape
    assert d == 1024 and s % TM == 0 and s // MOBA_BLOCK == 8
    depth = norm_gain.shape[0]
    x2 = x.reshape(b * s, d)

    grp = np.arange(256) // HEAD_DIM
    e256 = jnp.asarray(grp[:, None] == grp[None, :], dtype=BF16)
    tri = jnp.asarray(np.tril(np.ones((256, 256), np.float32)), dtype=BF16)
    perm = np.concatenate([np.arange(HEAD_DIM) + HEAD_DIM * hh for hh in SWA_HEAD_ORDER])

    for layer in range(depth):
        j = layer // 2
        ng = norm_gain[layer][None, :]
        if layer % 2 == 0:
            w = jnp.pad(w_in_ab[j], ((0, 0), (0, 4096 + LANES - w_in_ab.shape[2]))).astype(BF16)
            bf = jnp.pad(b_forget[j].astype(F32), (0, LANES - b_forget.shape[1]))[None, :]
            qa, ka, va, ga, qb, kb, vb, gb, km = _in_ab(
                x2, ng, w, _tile_gain(moba_q_gain[j], 8), _tile_gain(moba_k_gain[j], 8),
                _tile_gain(fox_q_gain[j], 8), _tile_gain(fox_k_gain[j], 8), bf, e256, tri, seq=s)
            r3 = lambda a: a.reshape(b, s, a.shape[1])
            ya = _attn_ab(r3(qa), r3(ka), r3(va), r3(ga), km.reshape(b, 8, 512), moba=True)
            yb = _attn_ab(r3(qb), r3(kb), r3(vb), r3(gb), None, moba=False)
            x2 = _out_proj(x2, ya.reshape(b * s, 512), yb.reshape(b * s, 512), w_out_ab[j].astype(BF16))
        else:
            wj = w_in_cd[j]
            w = jnp.concatenate([wj[:, :2048], wj[:, 2048:2560][:, perm], wj[:, 2560:2816],
                                 wj[:, 2816:3328][:, perm]], axis=1).astype(BF16)
            qc, kc, vc, gc, qd, kd, vd, gd = _in_cd(
                x2, ng, w, _tile_gain(diff_q_gain[j], 8), _tile_gain(diff_k_gain[j], 8),
                _tile_gain(swa_q_gain[j], 8), _tile_gain(swa_k_gain[j], 4), e256, seq=s)
            r3 = lambda a: a.reshape(b, s, a.shape[1])
            lam_init = 0.8 - 0.6 * math.exp(-0.3 * layer)
            yc = _attn_diff(r3(qc), r3(kc), r3(vc), r3(gc), diff_lambda[j].astype(F32),
                            diff_subln_gain[j].astype(F32)[None, :], lam_init=lam_init)
            sinks = swa_sinks[j].astype(F32)[np.asarray(SWA_HEAD_ORDER)].reshape(4, 2, 1)
            yd = _attn_swa(r3(qd), r3(kd), r3(vd), r3(gd), jnp.broadcast_to(sinks, (4, 2, LANES)))
            wo = w_out_cd[j]
            wo = jnp.concatenate([wo[:512], wo[512:][perm]], axis=0).astype(BF16)
            x2 = _out_proj(x2, yc.reshape(b * s, 512), yd.reshape(b * s, 512), wo)
    return x2.reshape(b, s, d)
```

```python
import functools
import math

import numpy as np
import jax
import jax.numpy as jnp
from jax import lax
from jax.experimental import pallas as pl
from jax.experimental.pallas import tpu as pltpu

F32 = jnp.float32
BF16 = jnp.bfloat16

LANES = 128
HEAD_DIM = 64
HALF = 64
EPS = 1e-6
SCALE = HEAD_DIM ** -0.5
TM = 512
TQ = 256
TK = 256
SWA_BLOCK = 128
MOBA_BLOCK = 256
MOBA_TOPK = 3
MASK_NEG = -1e30
SEL_NEG = -1e9
LOGIT_SAFE = 32.0
VMEM_LIMIT = 56 * 1024 * 1024

F_ALIBI = 0
F_FOX = 0
F_SEL = 8
F_ONE = 0


def _alibi_slopes(n):
    s = [2.0 ** (-8.0 * (i + 1) / n) for i in range(n)]
    for v in s:
        m, _ = math.frexp(v)
        assert m == 0.5, "feature-lane ALiBi needs power-of-two slopes"
    return s


def _cparams(sem):
    return pltpu.CompilerParams(dimension_semantics=sem, vmem_limit_bytes=VMEM_LIMIT)


def _group_sums(sq, e_ref):
    outs = []
    e = e_ref[...]
    for c in range(0, sq.shape[1], 256):
        outs.append(jnp.dot(sq[:, c:c + 256].astype(BF16), e, preferred_element_type=F32))
    return outs[0] if len(outs) == 1 else jnp.concatenate(outs, axis=1)


def _head_norm(p, gain_row, e_ref):
    ssq = _group_sums(p * p, e_ref)
    return p * lax.rsqrt(ssq * (1.0 / HEAD_DIM) + EPS) * gain_row


def _lane_iota(rows):
    return lax.broadcasted_iota(jnp.int32, (rows, LANES), 1)


def _pos_features(i, tiles_per_seq):
    lane = _lane_iota(TM)
    pos = (i % tiles_per_seq) * TM + lax.broadcasted_iota(jnp.int32, (TM, LANES), 0)
    a64 = ((pos >> 6) << 6).astype(F32)
    a = (pos >> 6).astype(F32)
    b = (pos & 63).astype(F32)
    blk = pos >> 8
    out = []
    for j in (0, 1):
        base = (1 - j) * HALF + F_ALIBI
        qb = jnp.where(lane == base, -a64,
             jnp.where(lane == base + 1, -b,
             jnp.where(lane == base + 2, 64.0,
             jnp.where(lane == base + 3, 1.0, 0.0))))
        kf = jnp.where((lane == base) | (lane == base + 1), 1.0,
             jnp.where(lane == base + 2, a,
             jnp.where(lane == base + 3, b, 0.0)))
        sel = jnp.where(lane - ((1 - j) * HALF + F_SEL) == blk, 1.0, 0.0)
        out.append((qb, kf, sel))
    return lane, out


def _rmsnorm_rows(x, gain_row):
    return x * lax.rsqrt(jnp.mean(x * x, axis=-1, keepdims=True) + EPS) * gain_row


def _store_blocks(dst_ref, src, feats):
    lane = _lane_iota(src.shape[0])
    for p in range(src.shape[1] // LANES):
        pair = src[:, p * LANES:(p + 1) * LANES]
        for j in (0, 1):
            data = (lane >= HALF) if j == 1 else (lane < HALF)
            blk = jnp.where(data, pair, feats(p, j))
            dst_ref[:, (2 * p + j) * LANES:(2 * p + j + 1) * LANES] = blk.astype(BF16)


def _value_blocks(dst_ref, src):
    lane = _lane_iota(src.shape[0])
    for p in range(src.shape[1] // LANES):
        pair = src[:, p * LANES:(p + 1) * LANES]
        for j in (0, 1):
            data = (lane >= HALF) if j == 1 else (lane < HALF)
            one = jnp.where(lane == (1 - j) * HALF + F_ONE, 1.0, 0.0)
            dst_ref[:, (2 * p + j) * LANES:(2 * p + j + 1) * LANES] = jnp.where(data, pair, one).astype(BF16)


def _wide_value_blocks(dst_ref, src):
    lane = _lane_iota(src.shape[0])
    one = jnp.where(lane == F_ONE, 1.0, 0.0).astype(BF16)
    for p in range(src.shape[1] // LANES):
        dst_ref[:, 2 * p * LANES:(2 * p + 1) * LANES] = src[:, p * LANES:(p + 1) * LANES].astype(BF16)
        dst_ref[:, (2 * p + 1) * LANES:(2 * p + 2) * LANES] = one


def _in_ab_kernel(x_ref, ng_ref, w_ref, gqa_ref, gka_ref, gqb_ref, gkb_ref, bf_ref, e_ref, tri_ref,
                  qa_ref, ka_ref, va_ref, ga_ref, qb_ref, kb_ref, vb_ref, gb_ref, km_ref,
                  carry_ref, *, tiles_per_seq):
    i = pl.program_id(0)
    h = _rmsnorm_rows(x_ref[...], ng_ref[...]).astype(BF16)

    def proj(c0, width):
        return jnp.dot(h, w_ref[:, c0:c0 + width], preferred_element_type=F32)

    lane, pf = _pos_features(i, tiles_per_seq)
    slopes = _alibi_slopes(8)

    qa = _head_norm(proj(0, 512), gqa_ref[...] * SCALE, e_ref)
    _store_blocks(qa_ref, qa, lambda p, j: pf[j][0] * slopes[2 * p + j])
    ka = _head_norm(proj(512, 512), gka_ref[...], e_ref)
    _store_blocks(ka_ref, ka, lambda p, j: pf[j][1] + pf[j][2])
    km = [jnp.mean(ka[r * MOBA_BLOCK:(r + 1) * MOBA_BLOCK], axis=0, keepdims=True)
          for r in range(TM // MOBA_BLOCK)]
    km_ref[0] = jnp.concatenate(km, axis=0)
    _value_blocks(va_ref, proj(1024, 512))
    ga_ref[...] = proj(1536, 512).astype(BF16)

    z = proj(4096, LANES) + bf_ref[...]
    logf = jnp.minimum(z, 0.0) - jnp.log1p(jnp.exp(-jnp.abs(z)))

    @pl.when(i % tiles_per_seq == 0)
    def _():
        carry_ref[...] = jnp.zeros_like(carry_ref)

    carry = carry_ref[...]
    tri = tri_ref[...]
    parts = []
    for r in range(TM // 256):
        lf = logf[r * 256:(r + 1) * 256]
        hi = lf.astype(BF16)
        r1 = lf - hi.astype(F32)
        mid = r1.astype(BF16)
        lo = (r1 - mid.astype(F32)).astype(BF16)
        cs = (jnp.dot(tri, hi, preferred_element_type=F32)
              + jnp.dot(tri, mid, preferred_element_type=F32)
              + jnp.dot(tri, lo, preferred_element_type=F32)) + carry
        carry = cs[255:256, :]
        parts.append(cs)
    carry_ref[...] = carry
    c = jnp.concatenate(parts, axis=0)
    chi = c.astype(BF16).astype(F32)
    r1 = c - chi
    cmid = r1.astype(BF16).astype(F32)
    clo = r1 - cmid

    def bcast(arr, hh):
        return jnp.broadcast_to(arr[:, hh:hh + 1], (TM, LANES))

    def fox_q(p, j):
        hh, base = 2 * p + j, (1 - j) * HALF + F_FOX
        ones = jnp.where((lane >= base + 3) & (lane < base + 6), 1.0, 0.0)
        return jnp.where(lane == base, bcast(chi, hh),
               jnp.where(lane == base + 1, bcast(cmid, hh),
               jnp.where(lane == base + 2, bcast(clo, hh), ones)))

    def fox_k(p, j):
        hh, base = 2 * p + j, (1 - j) * HALF + F_FOX
        ones = jnp.where((lane >= base) & (lane < base + 3), 1.0, 0.0)
        return jnp.where(lane == base + 3, -bcast(chi, hh),
               jnp.where(lane == base + 4, -bcast(cmid, hh),
               jnp.where(lane == base + 5, -bcast(clo, hh), ones)))

    qb = _head_norm(proj(2048, 512), gqb_ref[...] * SCALE, e_ref)
    _store_blocks(qb_ref, qb, fox_q)
    kb = _head_norm(proj(2560, 512), gkb_ref[...], e_ref)
    _store_blocks(kb_ref, kb, fox_k)
    _value_blocks(vb_ref, proj(3072, 512))
    gb_ref[...] = proj(3584, 512).astype(BF16)


def _in_ab(x2, ng, w, gqa, gka, gqb, gkb, bf, e256, tri, *, seq):
    n = x2.shape[0]
    nt = n // TM
    row = lambda width: pl.BlockSpec((TM, width), lambda i: (i, 0))
    full = lambda a: pl.BlockSpec(a.shape, lambda i: (0,) * a.ndim)
    bf16 = lambda width: jax.ShapeDtypeStruct((n, width), BF16)
    return pl.pallas_call(
        functools.partial(_in_ab_kernel, tiles_per_seq=seq // TM),
        grid=(nt,),
        in_specs=[row(x2.shape[1]), full(ng), full(w), full(gqa), full(gka), full(gqb), full(gkb),
                  full(bf), full(e256), full(tri)],
        out_specs=[row(1024), row(1024), row(1024), row(512), row(1024), row(1024), row(1024), row(512),
                   pl.BlockSpec((1, TM // MOBA_BLOCK, 512), lambda i: (i, 0, 0))],
        out_shape=[bf16(1024), bf16(1024), bf16(1024), bf16(512), bf16(1024), bf16(1024), bf16(1024),
                   bf16(512), jax.ShapeDtypeStruct((nt, TM // MOBA_BLOCK, 512), F32)],
        scratch_shapes=[pltpu.VMEM((1, LANES), F32)],
        compiler_params=_cparams(("arbitrary",)),
        name="in_proj_ab",
    )(x2, ng, w, gqa, gka, gqb, gkb, bf, e256, tri)


def _in_cd_kernel(x_ref, ng_ref, w_ref, gqc_ref, gkc_ref, gqd_ref, gkd_ref, e_ref,
                  qc_ref, kc_ref, vc_ref, gc_ref, qd_ref, kd_ref, vd_ref, gd_ref, *, tiles_per_seq):
    i = pl.program_id(0)
    h = _rmsnorm_rows(x_ref[...], ng_ref[...]).astype(BF16)

    def proj(c0, width):
        return jnp.dot(h, w_ref[:, c0:c0 + width], preferred_element_type=F32)

    _, pf = _pos_features(i, tiles_per_seq)
    dslopes = _alibi_slopes(4)
    sslopes = _alibi_slopes(8)

    qc = _head_norm(proj(0, 512), gqc_ref[...] * SCALE, e_ref)
    _store_blocks(qc_ref, qc, lambda p, j: pf[j][0] * dslopes[p])
    kc = _head_norm(proj(512, 512), gkc_ref[...], e_ref)
    _store_blocks(kc_ref, kc, lambda p, j: pf[j][1])
    _wide_value_blocks(vc_ref, proj(1024, 512))
    gc_ref[...] = proj(1536, 512).astype(BF16)

    qd = _head_norm(proj(2048, 512), gqd_ref[...] * SCALE, e_ref)
    _store_blocks(qd_ref, qd, lambda p, j: pf[j][0] * sslopes[p + 4 * j])
    kv = proj(2560, 256)
    kd = _head_norm(kv, gkd_ref[...], e_ref)[:, :LANES]
    _store_blocks(kd_ref, kd, lambda p, j: pf[j][1])
    _value_blocks(vd_ref, kv[:, LANES:])
    gd_ref[...] = proj(2816, 512).astype(BF16)


def _in_cd(x2, ng, w, gqc, gkc, gqd, gkd, e256, *, seq):
    n = x2.shape[0]
    nt = n // TM
    row = lambda width: pl.BlockSpec((TM, width), lambda i: (i, 0))
    full = lambda a: pl.BlockSpec(a.shape, lambda i: (0,) * a.ndim)
    bf16 = lambda width: jax.ShapeDtypeStruct((n, width), BF16)
    return pl.pallas_call(
        functools.partial(_in_cd_kernel, tiles_per_seq=seq // TM),
        grid=(nt,),
        in_specs=[row(x2.shape[1]), full(ng), full(w), full(gqc), full(gkc), full(gqd), full(gkd),
                  full(e256)],
        out_specs=[row(1024), row(1024), row(1024), row(512), row(1024), row(256), row(256), row(512)],
        out_shape=[bf16(1024), bf16(1024), bf16(1024), bf16(512), bf16(1024), bf16(256), bf16(256),
                   bf16(512)],
        compiler_params=_cparams(("arbitrary",)),
        name="in_proj_cd",
    )(x2, ng, w, gqc, gkc, gqd, gkd, e256)


def _out_kernel(x_ref, ya_ref, yb_ref, w_ref, o_ref):
    half = ya_ref.shape[1]
    o_ref[...] = (x_ref[...]
                  + jnp.dot(ya_ref[...], w_ref[:half, :], preferred_element_type=F32)
                  + jnp.dot(yb_ref[...], w_ref[half:, :], preferred_element_type=F32))


def _out_proj(x2, ya, yb, w):
    n, d = x2.shape
    row = lambda width: pl.BlockSpec((TM, width), lambda i: (i, 0))
    return pl.pallas_call(
        _out_kernel,
        grid=(n // TM,),
        in_specs=[row(d), row(ya.shape[1]), row(yb.shape[1]), pl.BlockSpec(w.shape, lambda i: (0, 0))],
        out_specs=row(d),
        out_shape=jax.ShapeDtypeStruct((n, d), F32),
        input_output_aliases={0: 0},
        compiler_params=_cparams(("arbitrary",)),
        name="out_proj",
    )(x2, ya, yb, w)


def _qk(q, k):
    return lax.dot_general(q, k, (((1,), (1,)), ((), ())), preferred_element_type=F32)


def _silu(g):
    return g * (1.0 / (1.0 + jnp.exp(-g)))


def _causal_mask():
    row = lax.broadcasted_iota(jnp.int32, (TQ, TK), 0)
    col = lax.broadcasted_iota(jnp.int32, (TQ, TK), 1)
    return row >= col


def _causal_tile(qj, k_ref, v_ref, kc0, vc0, vw, qi, causal, sub_max):
    n = (qi + 1) * TK
    s = _qk(qj, k_ref[0, :n, kc0:kc0 + LANES])
    diag = jnp.where(causal, s[:, n - TK:], MASK_NEG)
    s = diag if qi == 0 else jnp.concatenate([s[:, :n - TK], diag], axis=1)
    if sub_max:
        s = s - jnp.max(s, axis=-1, keepdims=True)
    p = jnp.exp(s).astype(BF16)
    return jnp.dot(p, v_ref[0, :n, vc0:vc0 + vw], preferred_element_type=F32)


def _moba_select(qj, km16, j, qi):
    if qi == 0:
        return qj
    g2x = _qk(km16, qj)
    gate = g2x[:8] + g2x[8:]
    n_iota = lax.broadcasted_iota(jnp.int32, (8, TQ), 0)
    cnt = jnp.zeros((8, TQ), F32)
    for n2 in range(qi):
        g2 = jnp.broadcast_to(gate[n2:n2 + 1, :], (8, TQ))
        beats = (g2 > gate) | ((g2 == gate) & (n_iota > n2))
        cnt = cnt + jnp.where(beats, 1.0, 0.0)
    keep = (cnt < MOBA_TOPK) | (n_iota >= qi)
    selb = jnp.where(keep, 0.0, SEL_NEG)
    base = (1 - j) * HALF + F_SEL
    planes = jnp.concatenate([jnp.zeros((base, TQ), F32), selb,
                              jnp.zeros((LANES - base - 8, TQ), F32)], axis=0)
    feat = planes.T
    lane = _lane_iota(TQ)
    return jnp.where((lane >= base) & (lane < base + 8), feat.astype(BF16), qj)


def _attn_ab_kernel(*refs, moba, nq, sub_max):
    if moba:
        q_ref, k_ref, v_ref, g_ref, km_ref, o_ref = refs
    else:
        q_ref, k_ref, v_ref, g_ref, o_ref = refs
    lane = _lane_iota(TQ)
    causal = _causal_mask()
    km16 = []
    if moba:
        lane8 = lax.broadcasted_iota(jnp.int32, (8, LANES), 1)
        for j in (0, 1):
            km = jnp.where((lane8 >= HALF) if j == 1 else (lane8 < HALF), km_ref[0], 0.0)
            hi = km.astype(BF16)
            lo = (km - hi.astype(F32)).astype(BF16)
            km16.append(jnp.concatenate([hi, lo], axis=0))
    for qi in range(nq):
        r0 = qi * TQ
        outs = []
        for j in (0, 1):
            c0 = j * LANES
            qj = q_ref[0, r0:r0 + TQ, c0:c0 + LANES]
            if moba:
                qj = _moba_select(qj, km16[j], j, qi)
            acc = _causal_tile(qj, k_ref, v_ref, c0, c0, LANES, qi, causal, sub_max)
            lcol = (1 - j) * HALF + F_ONE
            outs.append(acc / jnp.broadcast_to(acc[:, lcol:lcol + 1], (TQ, LANES)))
        o = jnp.where(lane < HALF, outs[0], outs[1])
        o_ref[0, r0:r0 + TQ, :] = (o * _silu(g_ref[0, r0:r0 + TQ, :].astype(F32))).astype(BF16)


def _pair_spec(s, width):
    return pl.BlockSpec((1, s, width), lambda bi, p: (bi, 0, p))


def _attn_ab(q, k, v, g, km, *, moba, sub_max):
    b, s, _ = q.shape
    in_specs = [_pair_spec(s, 2 * LANES), _pair_spec(s, 2 * LANES), _pair_spec(s, 2 * LANES),
                _pair_spec(s, LANES)]
    args = [q, k, v, g]
    if moba:
        in_specs.append(pl.BlockSpec((1, 8, LANES), lambda bi, p: (bi, 0, p)))
        args.append(km)
    return pl.pallas_call(
        functools.partial(_attn_ab_kernel, moba=moba, nq=s // TQ, sub_max=sub_max),
        grid=(b, 4),
        in_specs=in_specs,
        out_specs=_pair_spec(s, LANES),
        out_shape=jax.ShapeDtypeStruct((b, s, 4 * LANES), BF16),
        compiler_params=_cparams(("arbitrary", "arbitrary")),
        name=("attn_moba" if moba else "attn_fox") + ("_submax" if sub_max else ""),
    )(*args)


def _attn_diff_kernel(q_ref, k_ref, v_ref, g_ref, lam_ref, sg_ref, o_ref, *, lam_init, nq, sub_max):
    lp = lam_ref[...]
    s1 = jnp.sum(lp[0:1] * lp[1:2], axis=-1, keepdims=True)
    s2 = jnp.sum(lp[2:3] * lp[3:4], axis=-1, keepdims=True)
    lam = jnp.exp(s1) - jnp.exp(s2) + lam_init
    causal = _causal_mask()
    for qi in range(nq):
        r0 = qi * TQ
        outs = []
        for j in (0, 1):
            c0 = j * LANES
            qj = q_ref[0, r0:r0 + TQ, c0:c0 + LANES]
            acc = _causal_tile(qj, k_ref, v_ref, c0, 0, 2 * LANES, qi, causal, sub_max)
            outs.append(acc[:, :LANES] / jnp.broadcast_to(acc[:, LANES:LANES + 1], (TQ, LANES)))
        oc = outs[0] - lam * outs[1]
        oc = _rmsnorm_rows(oc, sg_ref[...]) * (1.0 - lam_init)
        o_ref[0, r0:r0 + TQ, :] = (oc * _silu(g_ref[0, r0:r0 + TQ, :].astype(F32))).astype(BF16)


def _attn_diff(q, k, v, g, lam, sg, *, lam_init, sub_max):
    b, s, _ = q.shape
    return pl.pallas_call(
        functools.partial(_attn_diff_kernel, lam_init=lam_init, nq=s // TQ, sub_max=sub_max),
        grid=(b, 4),
        in_specs=[_pair_spec(s, 2 * LANES), _pair_spec(s, 2 * LANES), _pair_spec(s, 2 * LANES),
                  _pair_spec(s, LANES),
                  pl.BlockSpec(lam.shape, lambda bi, p: (0, 0)),
                  pl.BlockSpec(sg.shape, lambda bi, p: (0, 0))],
        out_specs=_pair_spec(s, LANES),
        out_shape=jax.ShapeDtypeStruct((b, s, 4 * LANES), BF16),
        compiler_params=_cparams(("arbitrary", "arbitrary")),
        name="attn_diff" + ("_submax" if sub_max else ""),
    )(q, k, v, g, lam, sg)


def _attn_swa_kernel(q_ref, k_ref, v_ref, g_ref, sink_ref, o_ref, *, nq, sub_max):
    lane = _lane_iota(TQ)
    masks = {}
    for qi in range(nq):
        t0 = qi * TQ
        start = max(t0 - SWA_BLOCK, 0)
        nkeys = t0 + TQ - start
        if (nkeys, t0 - start) not in masks:
            row = lax.broadcasted_iota(jnp.int32, (TQ, nkeys), 0)
            col = lax.broadcasted_iota(jnp.int32, (TQ, nkeys), 1)
            rel = row - col + (t0 - start)
            masks[(nkeys, t0 - start)] = (rel >= 0) & (rel < SWA_BLOCK)
        ok = masks[(nkeys, t0 - start)]
        outs = []
        for j in (0, 1):
            c0 = j * LANES
            qj = q_ref[0, t0:t0 + TQ, c0:c0 + LANES]
            s = jnp.where(ok, _qk(qj, k_ref[0, start:start + nkeys, c0:c0 + LANES]), MASK_NEG)
            sink = sink_ref[0, j:j + 1, :]
            if sub_max:
                m = jnp.maximum(jnp.max(s, axis=-1, keepdims=True), sink[:, 0:1])
                s, sink = s - m, sink - m
            p = jnp.exp(s).astype(BF16)
            acc = jnp.dot(p, v_ref[0, start:start + nkeys, c0:c0 + LANES], preferred_element_type=F32)
            lcol = (1 - j) * HALF + F_ONE
            l = jnp.broadcast_to(acc[:, lcol:lcol + 1], (TQ, LANES)) + jnp.exp(sink)
            outs.append(acc / l)
        o = jnp.where(lane < HALF, outs[0], outs[1])
        o_ref[0, t0:t0 + TQ, :] = (o * _silu(g_ref[0, t0:t0 + TQ, :].astype(F32))).astype(BF16)


def _attn_swa(q, k, v, g, sinks, *, sub_max):
    b, s, _ = q.shape
    kv_spec = pl.BlockSpec((1, s, 2 * LANES), lambda bi, p: (bi, 0, 0))
    return pl.pallas_call(
        functools.partial(_attn_swa_kernel, nq=s // TQ, sub_max=sub_max),
        grid=(b, 4),
        in_specs=[_pair_spec(s, 2 * LANES), kv_spec, kv_spec, _pair_spec(s, LANES),
                  pl.BlockSpec((1, 2, LANES), lambda bi, p: (p, 0, 0))],
        out_specs=_pair_spec(s, LANES),
        out_shape=jax.ShapeDtypeStruct((b, s, 4 * LANES), BF16),
        compiler_params=_cparams(("arbitrary", "arbitrary")),
        name="attn_swa" + ("_submax" if sub_max else ""),
    )(q, k, v, g, sinks)


SWA_HEAD_ORDER = (0, 4, 1, 5, 2, 6, 3, 7)


def _tile_gain(g, reps):
    return jnp.tile(g.astype(F32), reps)[None, :]


def _logit_bound(gq, gk):
    return HEAD_DIM * SCALE * jnp.max(jnp.abs(gq)) * jnp.max(jnp.abs(gk))


def _guarded(bound, fn, *args):
    return lax.cond(bound <= LOGIT_SAFE, functools.partial(fn, sub_max=False),
                    functools.partial(fn, sub_max=True), *args)


def kernel(x, norm_gain, w_in_ab, b_forget, moba_q_gain, moba_k_gain, fox_q_gain, fox_k_gain, w_out_ab,
           w_in_cd, diff_q_gain, diff_k_gain, diff_lambda, diff_subln_gain, swa_q_gain, swa_k_gain,
           swa_sinks, w_out_cd):
    b, s, d = x.shape
    assert d == 1024 and s % TM == 0 and s // MOBA_BLOCK == 8
    depth = norm_gain.shape[0]
    x2 = x.reshape(b * s, d)

    grp = np.arange(256) // HEAD_DIM
    e256 = jnp.asarray(grp[:, None] == grp[None, :], dtype=BF16)
    tri = jnp.asarray(np.tril(np.ones((256, 256), np.float32)), dtype=BF16)
    perm = np.concatenate([np.arange(HEAD_DIM) + HEAD_DIM * hh for hh in SWA_HEAD_ORDER])

    for layer in range(depth):
        j = layer // 2
        ng = norm_gain[layer][None, :]
        if layer % 2 == 0:
            w = jnp.pad(w_in_ab[j], ((0, 0), (0, 4096 + LANES - w_in_ab.shape[2]))).astype(BF16)
            bf = jnp.pad(b_forget[j].astype(F32), (0, LANES - b_forget.shape[1]))[None, :]
            qa, ka, va, ga, qb, kb, vb, gb, km = _in_ab(
                x2, ng, w, _tile_gain(moba_q_gain[j], 8), _tile_gain(moba_k_gain[j], 8),
                _tile_gain(fox_q_gain[j], 8), _tile_gain(fox_k_gain[j], 8), bf, e256, tri, seq=s)
            r3 = lambda a: a.reshape(b, s, a.shape[1])
            ya = _guarded(_logit_bound(moba_q_gain[j], moba_k_gain[j]), functools.partial(_attn_ab, moba=True),
                          r3(qa), r3(ka), r3(va), r3(ga), km.reshape(b, 8, 512))
            yb = _guarded(_logit_bound(fox_q_gain[j], fox_k_gain[j]), functools.partial(_attn_ab, moba=False),
                          r3(qb), r3(kb), r3(vb), r3(gb), None)
            x2 = _out_proj(x2, ya.reshape(b * s, 512), yb.reshape(b * s, 512), w_out_ab[j].astype(BF16))
        else:
            wj = w_in_cd[j]
            w = jnp.concatenate([wj[:, :2048], wj[:, 2048:2560][:, perm], wj[:, 2560:2816],
                                 wj[:, 2816:3328][:, perm]], axis=1).astype(BF16)
            qc, kc, vc, gc, qd, kd, vd, gd = _in_cd(
                x2, ng, w, _tile_gain(diff_q_gain[j], 8), _tile_gain(diff_k_gain[j], 8),
                _tile_gain(swa_q_gain[j], 8), _tile_gain(swa_k_gain[j], 4), e256, seq=s)
            r3 = lambda a: a.reshape(b, s, a.shape[1])
            lam_init = 0.8 - 0.6 * math.exp(-0.3 * layer)
            yc = _guarded(_logit_bound(diff_q_gain[j], diff_k_gain[j]),
                          functools.partial(_attn_diff, lam_init=lam_init),
                          r3(qc), r3(kc), r3(vc), r3(gc), diff_lambda[j].astype(F32),
                          diff_subln_gain[j].astype(F32)[None, :])
            sinks = swa_sinks[j].astype(F32)[np.asarray(SWA_HEAD_ORDER)].reshape(4, 2, 1)
            swa_bound = jnp.maximum(_logit_bound(swa_q_gain[j], swa_k_gain[j]), jnp.max(sinks))
            yd = _guarded(swa_bound, _attn_swa,
                          r3(qd), r3(kd), r3(vd), r3(gd), jnp.broadcast_to(sinks, (4, 2, LANES)))
            wo = w_out_cd[j]
            wo = jnp.concatenate([wo[:512], wo[512:][perm]], axis=0).astype(BF16)
            x2 = _out_proj(x2, yc.reshape(b * s, 512), yd.reshape(b * s, 512), wo)
    return x2.reshape(b, s, d)
```

```python
import functools
import math

import numpy as np
import jax
import jax.numpy as jnp
from jax import lax
from jax.experimental import pallas as pl
from jax.experimental.pallas import tpu as pltpu

F32 = jnp.float32
BF16 = jnp.bfloat16

LANES = 128
BF16_ROWS = 16
HEAD_DIM = 64
HALF = 64
EPS = 1e-6
SCALE = HEAD_DIM ** -0.5
TM = 512
TQ = 512
TK = 256
SWA_BLOCK = 128
MOBA_BLOCK = 256
MOBA_TOPK = 3
MASK_NEG = -1e30
SEL_NEG = -1e9
LOGIT_SAFE = 32.0
VMEM_LIMIT = 56 * 1024 * 1024

F_ALIBI = 0
F_FOX = 0


def _alibi_slopes(n):
    s = [2.0 ** (-8.0 * (i + 1) / n) for i in range(n)]
    for v in s:
        m, _ = math.frexp(v)
        assert m == 0.5, "feature-lane ALiBi needs power-of-two slopes"
    return s


def _cparams(sem):
    return pltpu.CompilerParams(dimension_semantics=sem, vmem_limit_bytes=VMEM_LIMIT)


def _nt(a, b):
    return lax.dot_general(a, b, (((1,), (1,)), ((), ())), preferred_element_type=F32)


def _group_sums(sq, e_ref):
    width = sq.shape[1]
    if width == LANES:
        return jnp.dot(sq.astype(BF16), e_ref[:LANES, :LANES], preferred_element_type=F32)
    e = e_ref[...]
    outs = [jnp.dot(sq[:, c:c + 256].astype(BF16), e, preferred_element_type=F32)
            for c in range(0, width, 256)]
    return outs[0] if len(outs) == 1 else jnp.concatenate(outs, axis=1)


def _head_norm(p, gain_row, e_ref):
    ssq = _group_sums(p * p, e_ref)
    return p * lax.rsqrt(ssq * (1.0 / HEAD_DIM) + EPS) * gain_row


def _lane_iota(rows):
    return lax.broadcasted_iota(jnp.int32, (rows, LANES), 1)


def _pos_features(i, tiles_per_seq):
    lane = _lane_iota(TM)
    pos = (i % tiles_per_seq) * TM + lax.broadcasted_iota(jnp.int32, (TM, LANES), 0)
    a64 = ((pos >> 6) << 6).astype(F32)
    a = (pos >> 6).astype(F32)
    b = (pos & 63).astype(F32)
    out = []
    for j in (0, 1):
        base = (1 - j) * HALF + F_ALIBI
        qb = jnp.where(lane == base, -a64,
             jnp.where(lane == base + 1, -b,
             jnp.where(lane == base + 2, 64.0,
             jnp.where(lane == base + 3, 1.0, 0.0))))
        kf = jnp.where((lane == base) | (lane == base + 1), 1.0,
             jnp.where(lane == base + 2, a,
             jnp.where(lane == base + 3, b, 0.0)))
        out.append((qb, kf))
    return lane, out


def _rmsnorm_rows(x, gain_row):
    return x * lax.rsqrt(jnp.mean(x * x, axis=-1, keepdims=True) + EPS) * gain_row


def _store_blocks(dst_ref, src, feats):
    lane = _lane_iota(src.shape[0])
    for p in range(src.shape[1] // LANES):
        pair = src[:, p * LANES:(p + 1) * LANES]
        for j in (0, 1):
            data = (lane >= HALF) if j == 1 else (lane < HALF)
            blk = jnp.where(data, pair, feats(p, j))
            dst_ref[:, (2 * p + j) * LANES:(2 * p + j + 1) * LANES] = blk.astype(BF16)


def _in_ab_kernel(x_ref, ng_ref, w_ref, wvt_ref, gqa_ref, gka_ref, gqb_ref, gkb_ref, bf_ref, e_ref, tri_ref,
                  qa_ref, ka_ref, ga_ref, qb_ref, kb_ref, gb_ref, vt_ref, km_ref,
                  carry_ref, *, tiles_per_seq):
    i = pl.program_id(0)
    h = _rmsnorm_rows(x_ref[...], ng_ref[...]).astype(BF16)

    def proj(c0, width):
        return jnp.dot(h, w_ref[:, c0:c0 + width], preferred_element_type=F32)

    lane, pf = _pos_features(i, tiles_per_seq)
    slopes = _alibi_slopes(8)
    vt_ref[...] = _nt(wvt_ref[...], h).astype(BF16)

    qa = _head_norm(proj(0, 512), gqa_ref[...] * SCALE, e_ref)
    _store_blocks(qa_ref, qa, lambda p, j: pf[j][0] * slopes[2 * p + j])
    ka = _head_norm(proj(512, 512), gka_ref[...], e_ref)
    _store_blocks(ka_ref, ka, lambda p, j: pf[j][1])
    km = [jnp.mean(ka[r * MOBA_BLOCK:(r + 1) * MOBA_BLOCK], axis=0, keepdims=True)
          for r in range(TM // MOBA_BLOCK)]
    km_ref[0] = jnp.concatenate(km, axis=0)
    ga_ref[...] = proj(1024, 512).astype(BF16)

    z = proj(3072, LANES) + bf_ref[...]
    logf = jnp.minimum(z, 0.0) - jnp.log1p(jnp.exp(-jnp.abs(z)))

    @pl.when(i % tiles_per_seq == 0)
    def _():
        carry_ref[...] = jnp.zeros_like(carry_ref)

    carry = carry_ref[...]
    tri = tri_ref[...]
    parts = []
    for r in range(TM // 256):
        lf = logf[r * 256:(r + 1) * 256]
        hi = lf.astype(BF16)
        r1 = lf - hi.astype(F32)
        mid = r1.astype(BF16)
        lo = (r1 - mid.astype(F32)).astype(BF16)
        cs = (jnp.dot(tri, hi, preferred_element_type=F32)
              + jnp.dot(tri, mid, preferred_element_type=F32)
              + jnp.dot(tri, lo, preferred_element_type=F32)) + carry
        carry = cs[255:256, :]
        parts.append(cs)
    carry_ref[...] = carry
    c = jnp.concatenate(parts, axis=0)
    chi = c.astype(BF16).astype(F32)
    r1 = c - chi
    cmid = r1.astype(BF16).astype(F32)
    clo = r1 - cmid

    def bcast(arr, hh):
        return jnp.broadcast_to(arr[:, hh:hh + 1], (TM, LANES))

    def fox_q(p, j):
        hh, base = 2 * p + j, (1 - j) * HALF + F_FOX
        ones = jnp.where((lane >= base + 3) & (lane < base + 6), 1.0, 0.0)
        return jnp.where(lane == base, bcast(chi, hh),
               jnp.where(lane == base + 1, bcast(cmid, hh),
               jnp.where(lane == base + 2, bcast(clo, hh), ones)))

    def fox_k(p, j):
        hh, base = 2 * p + j, (1 - j) * HALF + F_FOX
        ones = jnp.where((lane >= base) & (lane < base + 3), 1.0, 0.0)
        return jnp.where(lane == base + 3, -bcast(chi, hh),
               jnp.where(lane == base + 4, -bcast(cmid, hh),
               jnp.where(lane == base + 5, -bcast(clo, hh), ones)))

    qb = _head_norm(proj(1536, 512), gqb_ref[...] * SCALE, e_ref)
    _store_blocks(qb_ref, qb, fox_q)
    kb = _head_norm(proj(2048, 512), gkb_ref[...], e_ref)
    _store_blocks(kb_ref, kb, fox_k)
    gb_ref[...] = proj(2560, 512).astype(BF16)


def _in_ab(x2, ng, w, wvt, gqa, gka, gqb, gkb, bf, e256, tri, *, seq):
    n = x2.shape[0]
    nt = n // TM
    row = lambda width: pl.BlockSpec((TM, width), lambda i: (i, 0))
    full = lambda a: pl.BlockSpec(a.shape, lambda i: (0,) * a.ndim)
    bf16 = lambda width: jax.ShapeDtypeStruct((n, width), BF16)
    return pl.pallas_call(
        functools.partial(_in_ab_kernel, tiles_per_seq=seq // TM),
        grid=(nt,),
        in_specs=[row(x2.shape[1]), full(ng), full(w), full(wvt), full(gqa), full(gka), full(gqb),
                  full(gkb), full(bf), full(e256), full(tri)],
        out_specs=[row(1024), row(1024), row(512), row(1024), row(1024), row(512),
                   pl.BlockSpec((wvt.shape[0], TM), lambda i: (0, i)),
                   pl.BlockSpec((1, TM // MOBA_BLOCK, 512), lambda i: (i, 0, 0))],
        out_shape=[bf16(1024), bf16(1024), bf16(512), bf16(1024), bf16(1024), bf16(512),
                   jax.ShapeDtypeStruct((wvt.shape[0], n), BF16),
                   jax.ShapeDtypeStruct((nt, TM // MOBA_BLOCK, 512), F32)],
        scratch_shapes=[pltpu.VMEM((1, LANES), F32)],
        compiler_params=_cparams(("arbitrary",)),
        name="in_proj_ab",
    )(x2, ng, w, wvt, gqa, gka, gqb, gkb, bf, e256, tri)


def _in_cd_kernel(x_ref, ng_ref, w_ref, wvt_ref, gqc_ref, gkc_ref, gqd_ref, gkd_ref, e_ref,
                  qc_ref, kc_ref, gc_ref, qd_ref, kd_ref, gd_ref, vt_ref, *, tiles_per_seq):
    i = pl.program_id(0)
    h = _rmsnorm_rows(x_ref[...], ng_ref[...]).astype(BF16)

    def proj(c0, width):
        return jnp.dot(h, w_ref[:, c0:c0 + width], preferred_element_type=F32)

    _, pf = _pos_features(i, tiles_per_seq)
    dslopes = _alibi_slopes(4)
    sslopes = _alibi_slopes(8)
    vt_ref[...] = _nt(wvt_ref[...], h).astype(BF16)

    qc = _head_norm(proj(0, 512), gqc_ref[...] * SCALE, e_ref)
    _store_blocks(qc_ref, qc, lambda p, j: pf[j][0] * dslopes[p])
    kc = _head_norm(proj(512, 512), gkc_ref[...], e_ref)
    _store_blocks(kc_ref, kc, lambda p, j: pf[j][1])
    gc_ref[...] = proj(1024, 512).astype(BF16)

    qd = _head_norm(proj(1536, 512), gqd_ref[...] * SCALE, e_ref)
    _store_blocks(qd_ref, qd, lambda p, j: pf[j][0] * sslopes[p + 4 * j])
    kd = _head_norm(proj(2048, LANES), gkd_ref[...], e_ref)
    _store_blocks(kd_ref, kd, lambda p, j: pf[j][1])
    gd_ref[...] = proj(2176, 512).astype(BF16)


def _in_cd(x2, ng, w, wvt, gqc, gkc, gqd, gkd, e256, *, seq):
    n = x2.shape[0]
    nt = n // TM
    row = lambda width: pl.BlockSpec((TM, width), lambda i: (i, 0))
    full = lambda a: pl.BlockSpec(a.shape, lambda i: (0,) * a.ndim)
    bf16 = lambda width: jax.ShapeDtypeStruct((n, width), BF16)
    return pl.pallas_call(
        functools.partial(_in_cd_kernel, tiles_per_seq=seq // TM),
        grid=(nt,),
        in_specs=[row(x2.shape[1]), full(ng), full(w), full(wvt), full(gqc), full(gkc), full(gqd),
                  full(gkd), full(e256)],
        out_specs=[row(1024), row(1024), row(512), row(1024), row(256), row(512),
                   pl.BlockSpec((wvt.shape[0], TM), lambda i: (0, i))],
        out_shape=[bf16(1024), bf16(1024), bf16(512), bf16(1024), bf16(256), bf16(512),
                   jax.ShapeDtypeStruct((wvt.shape[0], n), BF16)],
        compiler_params=_cparams(("arbitrary",)),
        name="in_proj_cd",
    )(x2, ng, w, wvt, gqc, gkc, gqd, gkd, e256)


def _out_kernel(x_ref, ya_ref, yb_ref, w_ref, o_ref):
    half = ya_ref.shape[1]
    o_ref[...] = (x_ref[...]
                  + jnp.dot(ya_ref[...], w_ref[:half, :], preferred_element_type=F32)
                  + jnp.dot(yb_ref[...], w_ref[half:, :], preferred_element_type=F32))


def _out_proj(x2, ya, yb, w):
    n, d = x2.shape
    row = lambda width: pl.BlockSpec((TM, width), lambda i: (i, 0))
    return pl.pallas_call(
        _out_kernel,
        grid=(n // TM,),
        in_specs=[row(d), row(ya.shape[1]), row(yb.shape[1]), pl.BlockSpec(w.shape, lambda i: (0, 0))],
        out_specs=row(d),
        out_shape=jax.ShapeDtypeStruct((n, d), F32),
        input_output_aliases={0: 0},
        compiler_params=_cparams(("arbitrary",)),
        name="out_proj",
    )(x2, ya, yb, w)


def _silu(g):
    return g * (1.0 / (1.0 + jnp.exp(-g)))


def _attend(st, vt, sub_max):
    if sub_max:
        st = st - jnp.max(st, axis=0, keepdims=True)
    pt = jnp.exp(st).astype(BF16)
    lhs = jnp.concatenate([vt, jnp.ones((BF16_ROWS, vt.shape[1]), BF16)], axis=0)
    return jnp.dot(lhs, pt, preferred_element_type=F32)


def _normalized(ot, dims):
    return ot[:dims] / ot[dims:dims + 1]


def _causal_mask_t():
    key = lax.broadcasted_iota(jnp.int32, (TQ, TQ), 0)
    qry = lax.broadcasted_iota(jnp.int32, (TQ, TQ), 1)
    return key <= qry


def _own_block(qi):
    return qi * (TQ // TK) + lax.broadcasted_iota(jnp.int32, (1, TQ), 1) // TK


def _moba_bias(gate, qi):
    first = qi * (TQ // TK)
    own = _own_block(qi)
    n_iota = lax.broadcasted_iota(jnp.int32, (8, TQ), 0)
    cnt = jnp.zeros((8, TQ), F32)
    for n2 in range(first + TQ // TK - 1):
        g2 = jnp.broadcast_to(gate[n2:n2 + 1, :], (8, TQ))
        beats = (g2 > gate) | ((g2 == gate) & (n_iota > n2))
        if n2 >= first:
            beats = beats & (own > n2)
        cnt = cnt + jnp.where(beats, 1.0, 0.0)
    return jnp.where(cnt < MOBA_TOPK, 0.0, SEL_NEG)


def _attn_ab_kernel(*refs, moba, nq, sub_max):
    if moba:
        q_ref, k_ref, vt_ref, g_ref, km_ref, o_ref = refs
    else:
        q_ref, k_ref, vt_ref, g_ref, o_ref = refs
    causal = _causal_mask_t()
    km16 = []
    if moba:
        lane8 = lax.broadcasted_iota(jnp.int32, (8, LANES), 1)
        for j in (0, 1):
            km = jnp.where((lane8 >= HALF) if j == 1 else (lane8 < HALF), km_ref[0], 0.0)
            hi = km.astype(BF16)
            lo = (km - hi.astype(F32)).astype(BF16)
            km16.append(jnp.concatenate([hi, lo], axis=0))
    sub = TQ // TK
    for qi in range(nq):
        r0, n = qi * TQ, (qi + 1) * TQ
        first = qi * sub
        outs = []
        for j in (0, 1):
            c0 = j * LANES
            qj = q_ref[0, r0:r0 + TQ, c0:c0 + LANES]
            kk = k_ref[0, :n, c0:c0 + LANES]
            if moba and first + sub - 1 > MOBA_TOPK:
                st = _nt(jnp.concatenate([km16[j], kk], axis=0), qj)
                bias = _moba_bias(st[:8] + st[8:16], qi)
                st = st[16:]
                blocks = [st[nb * TK:(nb + 1) * TK] + bias[nb:nb + 1, :] for nb in range(first)]
                own = _own_block(qi)
                for r in range(sub):
                    blk = st[(first + r) * TK:(first + r + 1) * TK]
                    if r < sub - 1:
                        blk = blk + jnp.where(own > first + r, bias[first + r:first + r + 1, :], 0.0)
                    blocks.append(blk)
                st = jnp.concatenate(blocks, axis=0)
            else:
                st = _nt(kk, qj)
            tail = jnp.where(causal, st[n - TQ:], MASK_NEG)
            st = tail if qi == 0 else jnp.concatenate([st[:n - TQ], tail], axis=0)
            ot = _attend(st, vt_ref[j * HALF:(j + 1) * HALF, :n], sub_max)
            outs.append(_normalized(ot, HALF))
        o = jnp.concatenate(outs, axis=0).T
        o_ref[0, r0:r0 + TQ, :] = (o * _silu(g_ref[0, r0:r0 + TQ, :].astype(F32))).astype(BF16)


def _pair_spec(s, width):
    return pl.BlockSpec((1, s, width), lambda bi, p: (bi, 0, p))


def _vt_spec(s, row_block0, per_pair):
    if per_pair:
        return pl.BlockSpec((LANES, s), lambda bi, p: (row_block0 + p, bi))
    return pl.BlockSpec((LANES, s), lambda bi, p: (row_block0, bi))


def _attn_ab(q, k, vt, g, km, *, moba, sub_max):
    b, s, _ = q.shape
    in_specs = [_pair_spec(s, 2 * LANES), _pair_spec(s, 2 * LANES), _vt_spec(s, 0 if moba else 4, True),
                _pair_spec(s, LANES)]
    args = [q, k, vt, g]
    if moba:
        in_specs.append(pl.BlockSpec((1, 8, LANES), lambda bi, p: (bi, 0, p)))
        args.append(km)
    return pl.pallas_call(
        functools.partial(_attn_ab_kernel, moba=moba, nq=s // TQ, sub_max=sub_max),
        grid=(b, 4),
        in_specs=in_specs,
        out_specs=_pair_spec(s, LANES),
        out_shape=jax.ShapeDtypeStruct((b, s, 4 * LANES), BF16),
        compiler_params=_cparams(("arbitrary", "arbitrary")),
        name=("attn_moba" if moba else "attn_fox") + ("_submax" if sub_max else ""),
    )(*args)


def _attn_diff_kernel(q_ref, k_ref, vt_ref, g_ref, lam_ref, sg_ref, o_ref, *, lam_init, nq, sub_max):
    lp = lam_ref[...]
    s1 = jnp.sum(lp[0:1] * lp[1:2], axis=-1, keepdims=True)
    s2 = jnp.sum(lp[2:3] * lp[3:4], axis=-1, keepdims=True)
    lam = jnp.exp(s1) - jnp.exp(s2) + lam_init
    causal = _causal_mask_t()
    for qi in range(nq):
        r0, n = qi * TQ, (qi + 1) * TQ
        outs = []
        for j in (0, 1):
            c0 = j * LANES
            st = _nt(k_ref[0, :n, c0:c0 + LANES], q_ref[0, r0:r0 + TQ, c0:c0 + LANES])
            tail = jnp.where(causal, st[n - TQ:], MASK_NEG)
            st = tail if qi == 0 else jnp.concatenate([st[:n - TQ], tail], axis=0)
            outs.append(_normalized(_attend(st, vt_ref[:, :n], sub_max), LANES))
        oc = (outs[0] - lam * outs[1]).T
        oc = _rmsnorm_rows(oc, sg_ref[...]) * (1.0 - lam_init)
        o_ref[0, r0:r0 + TQ, :] = (oc * _silu(g_ref[0, r0:r0 + TQ, :].astype(F32))).astype(BF16)


def _attn_diff(q, k, vt, g, lam, sg, *, lam_init, sub_max):
    b, s, _ = q.shape
    return pl.pallas_call(
        functools.partial(_attn_diff_kernel, lam_init=lam_init, nq=s // TQ, sub_max=sub_max),
        grid=(b, 4),
        in_specs=[_pair_spec(s, 2 * LANES), _pair_spec(s, 2 * LANES), _vt_spec(s, 0, True),
                  _pair_spec(s, LANES),
                  pl.BlockSpec(lam.shape, lambda bi, p: (0, 0)),
                  pl.BlockSpec(sg.shape, lambda bi, p: (0, 0))],
        out_specs=_pair_spec(s, LANES),
        out_shape=jax.ShapeDtypeStruct((b, s, 4 * LANES), BF16),
        compiler_params=_cparams(("arbitrary", "arbitrary")),
        name="attn_diff" + ("_submax" if sub_max else ""),
    )(q, k, vt, g, lam, sg)


def _attn_swa_kernel(q_ref, k_ref, vt_ref, g_ref, sink_ref, o_ref, *, nq, sub_max):
    masks = {}
    for qi in range(nq):
        t0 = qi * TQ
        start = max(t0 - SWA_BLOCK, 0)
        nkeys = t0 + TQ - start
        if (nkeys, t0 - start) not in masks:
            key = lax.broadcasted_iota(jnp.int32, (nkeys, TQ), 0)
            qry = lax.broadcasted_iota(jnp.int32, (nkeys, TQ), 1)
            rel = qry - key + (t0 - start)
            masks[(nkeys, t0 - start)] = (rel >= 0) & (rel < SWA_BLOCK)
        ok = masks[(nkeys, t0 - start)]
        outs = []
        for j in (0, 1):
            c0 = j * LANES
            st = _nt(k_ref[0, start:start + nkeys, c0:c0 + LANES], q_ref[0, t0:t0 + TQ, c0:c0 + LANES])
            st = jnp.where(ok, st, MASK_NEG)
            sink = sink_ref[0, j:j + 1, :]
            if sub_max:
                m = jnp.maximum(jnp.max(st, axis=0, keepdims=True), sink)
                st, sink = st - m, sink - m
            ot = _attend(st, vt_ref[j * HALF:(j + 1) * HALF, start:start + nkeys], False)
            outs.append(ot[:HALF] / (ot[HALF:HALF + 1] + jnp.exp(sink)))
        o = jnp.concatenate(outs, axis=0).T
        o_ref[0, t0:t0 + TQ, :] = (o * _silu(g_ref[0, t0:t0 + TQ, :].astype(F32))).astype(BF16)


def _attn_swa(q, k, vt, g, sinks, *, sub_max):
    b, s, _ = q.shape
    return pl.pallas_call(
        functools.partial(_attn_swa_kernel, nq=s // TQ, sub_max=sub_max),
        grid=(b, 4),
        in_specs=[_pair_spec(s, 2 * LANES), pl.BlockSpec((1, s, 2 * LANES), lambda bi, p: (bi, 0, 0)),
                  _vt_spec(s, 4, False), _pair_spec(s, LANES),
                  pl.BlockSpec((1, 2, TQ), lambda bi, p: (p, 0, 0))],
        out_specs=_pair_spec(s, LANES),
        out_shape=jax.ShapeDtypeStruct((b, s, 4 * LANES), BF16),
        compiler_params=_cparams(("arbitrary", "arbitrary")),
        name="attn_swa" + ("_submax" if sub_max else ""),
    )(q, k, vt, g, sinks)


SWA_HEAD_ORDER = (0, 4, 1, 5, 2, 6, 3, 7)


def _tile_gain(g, reps):
    return jnp.tile(g.astype(F32), reps)[None, :]


def _logit_bound(gq, gk):
    return HEAD_DIM * SCALE * jnp.max(jnp.abs(gq)) * jnp.max(jnp.abs(gk))


def _guarded(bound, fn, *args):
    return lax.cond(bound <= LOGIT_SAFE, functools.partial(fn, sub_max=False),
                    functools.partial(fn, sub_max=True), *args)


def kernel(x, norm_gain, w_in_ab, b_forget, moba_q_gain, moba_k_gain, fox_q_gain, fox_k_gain, w_out_ab,
           w_in_cd, diff_q_gain, diff_k_gain, diff_lambda, diff_subln_gain, swa_q_gain, swa_k_gain,
           swa_sinks, w_out_cd):
    b, s, d = x.shape
    assert d == 1024 and s % TM == 0 and s // MOBA_BLOCK == 8
    depth = norm_gain.shape[0]
    x2 = x.reshape(b * s, d)

    grp = np.arange(256) // HEAD_DIM
    e256 = jnp.asarray(grp[:, None] == grp[None, :], dtype=BF16)
    tri = jnp.asarray(np.tril(np.ones((256, 256), np.float32)), dtype=BF16)
    perm = np.concatenate([np.arange(HEAD_DIM) + HEAD_DIM * hh for hh in SWA_HEAD_ORDER])
    r3 = lambda a: a.reshape(b, s, a.shape[1])

    for layer in range(depth):
        j = layer // 2
        ng = norm_gain[layer][None, :]
        if layer % 2 == 0:
            wj = w_in_ab[j]
            w = jnp.concatenate([wj[:, 0:1024], wj[:, 1536:2048], wj[:, 2048:3072], wj[:, 3584:4096],
                                 jnp.pad(wj[:, 4096:], ((0, 0), (0, LANES - (wj.shape[1] - 4096))))],
                                axis=1).astype(BF16)
            wvt = jnp.concatenate([wj[:, 1024:1536], wj[:, 3072:3584]], axis=1).T.astype(BF16)
            bf = jnp.pad(b_forget[j].astype(F32), (0, LANES - b_forget.shape[1]))[None, :]
            qa, ka, ga, qb, kb, gb, vt, km = _in_ab(
                x2, ng, w, wvt, _tile_gain(moba_q_gain[j], 8), _tile_gain(moba_k_gain[j], 8),
                _tile_gain(fox_q_gain[j], 8), _tile_gain(fox_k_gain[j], 8), bf, e256, tri, seq=s)
            ya = _guarded(_logit_bound(moba_q_gain[j], moba_k_gain[j]), functools.partial(_attn_ab, moba=True),
                          r3(qa), r3(ka), vt, r3(ga), km.reshape(b, 8, 512))
            yb = _guarded(_logit_bound(fox_q_gain[j], fox_k_gain[j]), functools.partial(_attn_ab, moba=False),
                          r3(qb), r3(kb), vt, r3(gb), None)
            x2 = _out_proj(x2, ya.reshape(b * s, 512), yb.reshape(b * s, 512), w_out_ab[j].astype(BF16))
        else:
            wj = w_in_cd[j]
            w = jnp.concatenate([wj[:, 0:1024], wj[:, 1536:2048], wj[:, 2048:2560][:, perm],
                                 wj[:, 2560:2688], wj[:, 2816:3328][:, perm]], axis=1).astype(BF16)
            wvt = jnp.concatenate([wj[:, 1024:1536], wj[:, 2688:2816]], axis=1).T.astype(BF16)
            qc, kc, gc, qd, kd, gd, vt = _in_cd(
                x2, ng, w, wvt, _tile_gain(diff_q_gain[j], 8), _tile_gain(diff_k_gain[j], 8),
                _tile_gain(swa_q_gain[j], 8), _tile_gain(swa_k_gain[j], 2), e256, seq=s)
            lam_init = 0.8 - 0.6 * math.exp(-0.3 * layer)
            yc = _guarded(_logit_bound(diff_q_gain[j], diff_k_gain[j]),
                          functools.partial(_attn_diff, lam_init=lam_init),
                          r3(qc), r3(kc), vt, r3(gc), diff_lambda[j].astype(F32),
                          diff_subln_gain[j].astype(F32)[None, :])
            sinks = swa_sinks[j].astype(F32)[np.asarray(SWA_HEAD_ORDER)].reshape(4, 2, 1)
            swa_bound = jnp.maximum(_logit_bound(swa_q_gain[j], swa_k_gain[j]), jnp.max(sinks))
            yd = _guarded(swa_bound, _attn_swa,
                          r3(qd), r3(kd), vt, r3(gd), jnp.broadcast_to(sinks, (4, 2, TQ)))
            wo = w_out_cd[j]
            wo = jnp.concatenate([wo[:512], wo[512:][perm]], axis=0).astype(BF16)
            x2 = _out_proj(x2, yc.reshape(b * s, 512), yd.reshape(b * s, 512), wo)
    return x2.reshape(b, s, d)
```

```python
import functools
import math

import numpy as np
import jax
import jax.numpy as jnp
from jax import lax
from jax.experimental import pallas as pl
from jax.experimental.pallas import tpu as pltpu

F32 = jnp.float32
BF16 = jnp.bfloat16

LANES = 128
BF16_ROWS = 16
HEAD_DIM = 64
HALF = 64
EPS = 1e-6
SCALE = HEAD_DIM ** -0.5
TM = 512
TQ = 512
TK = 256
PAIRS = 2
SWA_BLOCK = 128
MOBA_BLOCK = 256
MOBA_TOPK = 3
MASK_NEG = -1e30
SEL_NEG = -1e9
LOGIT_SAFE = 32.0
VMEM_LIMIT = 56 * 1024 * 1024

F_ALIBI = 0
F_FOX = 0


def _alibi_slopes(n):
    s = [2.0 ** (-8.0 * (i + 1) / n) for i in range(n)]
    for v in s:
        m, _ = math.frexp(v)
        assert m == 0.5, "feature-lane ALiBi needs power-of-two slopes"
    return s


def _cparams(sem):
    return pltpu.CompilerParams(dimension_semantics=sem, vmem_limit_bytes=VMEM_LIMIT)


def _nt(a, b):
    return lax.dot_general(a, b, (((1,), (1,)), ((), ())), preferred_element_type=F32)


def _group_sums(sq, e_ref):
    width = sq.shape[1]
    if width == LANES:
        return jnp.dot(sq.astype(BF16), e_ref[:LANES, :LANES], preferred_element_type=F32)
    e = e_ref[...]
    outs = [jnp.dot(sq[:, c:c + 256].astype(BF16), e, preferred_element_type=F32)
            for c in range(0, width, 256)]
    return outs[0] if len(outs) == 1 else jnp.concatenate(outs, axis=1)


def _head_norm(p, gain_row, e_ref):
    ssq = _group_sums(p * p, e_ref)
    return p * lax.rsqrt(ssq * (1.0 / HEAD_DIM) + EPS) * gain_row


def _lane_iota(rows):
    return lax.broadcasted_iota(jnp.int32, (rows, LANES), 1)


def _pos_features(i, tiles_per_seq):
    lane = _lane_iota(TM)
    pos = (i % tiles_per_seq) * TM + lax.broadcasted_iota(jnp.int32, (TM, LANES), 0)
    a64 = ((pos >> 6) << 6).astype(F32)
    a = (pos >> 6).astype(F32)
    b = (pos & 63).astype(F32)
    out = []
    for j in (0, 1):
        base = (1 - j) * HALF + F_ALIBI
        qb = jnp.where(lane == base, -a64,
             jnp.where(lane == base + 1, -b,
             jnp.where(lane == base + 2, 64.0,
             jnp.where(lane == base + 3, 1.0, 0.0))))
        kf = jnp.where((lane == base) | (lane == base + 1), 1.0,
             jnp.where(lane == base + 2, a,
             jnp.where(lane == base + 3, b, 0.0)))
        out.append((qb, kf))
    return lane, out


def _rmsnorm_rows(x, gain_row):
    return x * lax.rsqrt(jnp.mean(x * x, axis=-1, keepdims=True) + EPS) * gain_row


def _residual_tile(refs, fused, n_in):
    if not fused:
        return refs[0][...], refs[1:]
    x_ref, ya_ref, yb_ref, wo_ref = refs[:4]
    rest = refs[4:]
    half = ya_ref.shape[1]
    x = (x_ref[...]
         + jnp.dot(ya_ref[...], wo_ref[:half, :], preferred_element_type=F32)
         + jnp.dot(yb_ref[...], wo_ref[half:, :], preferred_element_type=F32))
    rest[n_in][...] = x
    return x, rest[:n_in] + rest[n_in + 1:]


def _residual_specs(x2, res):
    row = lambda width: pl.BlockSpec((TM, width), lambda i: (i, 0))
    if res is None:
        return [x2], [row(x2.shape[1])], [], []
    ya, yb, wo = res
    return ([x2, ya, yb, wo],
            [row(x2.shape[1]), row(ya.shape[1]), row(yb.shape[1]), pl.BlockSpec(wo.shape, lambda i: (0, 0))],
            [row(x2.shape[1])], [jax.ShapeDtypeStruct(x2.shape, F32)])


def _store_blocks(dst_ref, src, feats):
    lane = _lane_iota(src.shape[0])
    for p in range(src.shape[1] // LANES):
        pair = src[:, p * LANES:(p + 1) * LANES]
        for j in (0, 1):
            data = (lane >= HALF) if j == 1 else (lane < HALF)
            blk = jnp.where(data, pair, feats(p, j))
            dst_ref[:, (2 * p + j) * LANES:(2 * p + j + 1) * LANES] = blk.astype(BF16)


def _in_ab_kernel(*refs, tiles_per_seq, fused):
    x, refs = _residual_tile(refs, fused, 10)
    (ng_ref, w_ref, wvt_ref, gqa_ref, gka_ref, gqb_ref, gkb_ref, bf_ref, e_ref, tri_ref,
     qa_ref, ka_ref, ga_ref, qb_ref, kb_ref, gb_ref, vt_ref, km_ref, carry_ref) = refs
    i = pl.program_id(0)

    @pl.when(i % tiles_per_seq == 0)
    def _():
        carry_ref[...] = jnp.zeros_like(carry_ref)

    h = _rmsnorm_rows(x, ng_ref[...]).astype(BF16)

    def proj(c0, width):
        return jnp.dot(h, w_ref[:, c0:c0 + width], preferred_element_type=F32)

    lane, pf = _pos_features(i, tiles_per_seq)
    slopes = _alibi_slopes(8)
    vt_ref[...] = _nt(wvt_ref[...], h).astype(BF16)

    qa = _head_norm(proj(0, 512), gqa_ref[...] * SCALE, e_ref)
    _store_blocks(qa_ref, qa, lambda p, j: pf[j][0] * slopes[2 * p + j])
    ka = _head_norm(proj(512, 512), gka_ref[...], e_ref)
    _store_blocks(ka_ref, ka, lambda p, j: pf[j][1])
    km = [jnp.mean(ka[r * MOBA_BLOCK:(r + 1) * MOBA_BLOCK], axis=0, keepdims=True)
          for r in range(TM // MOBA_BLOCK)]
    km_ref[0] = jnp.concatenate(km, axis=0)
    ga_ref[...] = proj(1024, 512).astype(BF16)

    z = proj(3072, LANES) + bf_ref[...]
    logf = jnp.minimum(z, 0.0) - jnp.log1p(jnp.exp(-jnp.abs(z)))
    carry = carry_ref[...]
    tri = tri_ref[...]
    parts = []
    for r in range(TM // 256):
        lf = logf[r * 256:(r + 1) * 256]
        hi = lf.astype(BF16)
        r1 = lf - hi.astype(F32)
        mid = r1.astype(BF16)
        lo = (r1 - mid.astype(F32)).astype(BF16)
        cs = (jnp.dot(tri, hi, preferred_element_type=F32)
              + jnp.dot(tri, mid, preferred_element_type=F32)
              + jnp.dot(tri, lo, preferred_element_type=F32)) + carry
        carry = cs[255:256, :]
        parts.append(cs)
    carry_ref[...] = carry
    c = jnp.concatenate(parts, axis=0)
    chi = c.astype(BF16).astype(F32)
    r1 = c - chi
    cmid = r1.astype(BF16).astype(F32)
    clo = r1 - cmid

    def bcast(arr, hh):
        return jnp.broadcast_to(arr[:, hh:hh + 1], (TM, LANES))

    def fox_q(p, j):
        hh, base = 2 * p + j, (1 - j) * HALF + F_FOX
        ones = jnp.where((lane >= base + 3) & (lane < base + 6), 1.0, 0.0)
        return jnp.where(lane == base, bcast(chi, hh),
               jnp.where(lane == base + 1, bcast(cmid, hh),
               jnp.where(lane == base + 2, bcast(clo, hh), ones)))

    def fox_k(p, j):
        hh, base = 2 * p + j, (1 - j) * HALF + F_FOX
        ones = jnp.where((lane >= base) & (lane < base + 3), 1.0, 0.0)
        return jnp.where(lane == base + 3, -bcast(chi, hh),
               jnp.where(lane == base + 4, -bcast(cmid, hh),
               jnp.where(lane == base + 5, -bcast(clo, hh), ones)))

    qb = _head_norm(proj(1536, 512), gqb_ref[...] * SCALE, e_ref)
    _store_blocks(qb_ref, qb, fox_q)
    kb = _head_norm(proj(2048, 512), gkb_ref[...], e_ref)
    _store_blocks(kb_ref, kb, fox_k)
    gb_ref[...] = proj(2560, 512).astype(BF16)


def _in_ab(x2, res, ng, w, wvt, gqa, gka, gqb, gkb, bf, e256, tri, *, seq):
    n = x2.shape[0]
    nt = n // TM
    row = lambda width: pl.BlockSpec((TM, width), lambda i: (i, 0))
    full = lambda a: pl.BlockSpec(a.shape, lambda i: (0,) * a.ndim)
    bf16 = lambda width: jax.ShapeDtypeStruct((n, width), BF16)
    res_in, res_in_specs, res_out_specs, res_out_shapes = _residual_specs(x2, res)
    consts = [ng, w, wvt, gqa, gka, gqb, gkb, bf, e256, tri]
    return pl.pallas_call(
        functools.partial(_in_ab_kernel, tiles_per_seq=seq // TM, fused=res is not None),
        grid=(nt,),
        in_specs=res_in_specs + [full(a) for a in consts],
        out_specs=res_out_specs + [row(1024), row(1024), row(512), row(1024), row(1024), row(512),
                                   pl.BlockSpec((wvt.shape[0], TM), lambda i: (0, i)),
                                   pl.BlockSpec((1, TM // MOBA_BLOCK, 512), lambda i: (i, 0, 0))],
        out_shape=res_out_shapes + [bf16(1024), bf16(1024), bf16(512), bf16(1024), bf16(1024), bf16(512),
                                    jax.ShapeDtypeStruct((wvt.shape[0], n), BF16),
                                    jax.ShapeDtypeStruct((nt, TM // MOBA_BLOCK, 512), F32)],
        scratch_shapes=[pltpu.VMEM((1, LANES), F32)],
        compiler_params=_cparams(("arbitrary",)),
        name="in_proj_ab",
    )(*res_in, *consts)


def _in_cd_kernel(*refs, tiles_per_seq, fused):
    x, refs = _residual_tile(refs, fused, 8)
    (ng_ref, w_ref, wvt_ref, gqc_ref, gkc_ref, gqd_ref, gkd_ref, e_ref,
     qc_ref, kc_ref, gc_ref, qd_ref, kd_ref, gd_ref, vt_ref) = refs
    i = pl.program_id(0)
    h = _rmsnorm_rows(x, ng_ref[...]).astype(BF16)

    def proj(c0, width):
        return jnp.dot(h, w_ref[:, c0:c0 + width], preferred_element_type=F32)

    _, pf = _pos_features(i, tiles_per_seq)
    dslopes = _alibi_slopes(4)
    sslopes = _alibi_slopes(8)
    vt_ref[...] = _nt(wvt_ref[...], h).astype(BF16)

    qc = _head_norm(proj(0, 512), gqc_ref[...] * SCALE, e_ref)
    _store_blocks(qc_ref, qc, lambda p, j: pf[j][0] * dslopes[p])
    kc = _head_norm(proj(512, 512), gkc_ref[...], e_ref)
    _store_blocks(kc_ref, kc, lambda p, j: pf[j][1])
    gc_ref[...] = proj(1024, 512).astype(BF16)

    qd = _head_norm(proj(1536, 512), gqd_ref[...] * SCALE, e_ref)
    _store_blocks(qd_ref, qd, lambda p, j: pf[j][0] * sslopes[p + 4 * j])
    kd = _head_norm(proj(2048, LANES), gkd_ref[...], e_ref)
    _store_blocks(kd_ref, kd, lambda p, j: pf[j][1])
    gd_ref[...] = proj(2176, 512).astype(BF16)


def _in_cd(x2, res, ng, w, wvt, gqc, gkc, gqd, gkd, e256, *, seq):
    n = x2.shape[0]
    nt = n // TM
    row = lambda width: pl.BlockSpec((TM, width), lambda i: (i, 0))
    full = lambda a: pl.BlockSpec(a.shape, lambda i: (0,) * a.ndim)
    bf16 = lambda width: jax.ShapeDtypeStruct((n, width), BF16)
    res_in, res_in_specs, res_out_specs, res_out_shapes = _residual_specs(x2, res)
    consts = [ng, w, wvt, gqc, gkc, gqd, gkd, e256]
    return pl.pallas_call(
        functools.partial(_in_cd_kernel, tiles_per_seq=seq // TM, fused=res is not None),
        grid=(nt,),
        in_specs=res_in_specs + [full(a) for a in consts],
        out_specs=res_out_specs + [row(1024), row(1024), row(512), row(1024), row(256), row(512),
                                   pl.BlockSpec((wvt.shape[0], TM), lambda i: (0, i))],
        out_shape=res_out_shapes + [bf16(1024), bf16(1024), bf16(512), bf16(1024), bf16(256), bf16(512),
                                    jax.ShapeDtypeStruct((wvt.shape[0], n), BF16)],
        compiler_params=_cparams(("arbitrary",)),
        name="in_proj_cd",
    )(*res_in, *consts)


def _out_kernel(x_ref, ya_ref, yb_ref, w_ref, o_ref):
    half = ya_ref.shape[1]
    o_ref[...] = (x_ref[...]
                  + jnp.dot(ya_ref[...], w_ref[:half, :], preferred_element_type=F32)
                  + jnp.dot(yb_ref[...], w_ref[half:, :], preferred_element_type=F32))


def _out_proj(x2, ya, yb, w):
    n, d = x2.shape
    row = lambda width: pl.BlockSpec((TM, width), lambda i: (i, 0))
    return pl.pallas_call(
        _out_kernel,
        grid=(n // TM,),
        in_specs=[row(d), row(ya.shape[1]), row(yb.shape[1]), pl.BlockSpec(w.shape, lambda i: (0, 0))],
        out_specs=row(d),
        out_shape=jax.ShapeDtypeStruct((n, d), F32),
        compiler_params=_cparams(("arbitrary",)),
        name="out_proj",
    )(x2, ya, yb, w)


def _silu(g):
    return g * (1.0 / (1.0 + jnp.exp(-g)))


def _attend(st, vt, sub_max):
    if sub_max:
        st = st - jnp.max(st, axis=0, keepdims=True)
    pt = jnp.exp(st).astype(BF16)
    lhs = jnp.concatenate([vt, jnp.ones((BF16_ROWS, vt.shape[1]), BF16)], axis=0)
    return jnp.dot(lhs, pt, preferred_element_type=F32)


def _normalized(ot, dims):
    return ot[:dims] / ot[dims:dims + 1]


def _causal_mask_t():
    key = lax.broadcasted_iota(jnp.int32, (TQ, TQ), 0)
    qry = lax.broadcasted_iota(jnp.int32, (TQ, TQ), 1)
    return key <= qry


def _own_block(qi):
    return qi * (TQ // TK) + lax.broadcasted_iota(jnp.int32, (1, TQ), 1) // TK


def _moba_bias(gate, qi):
    first = qi * (TQ // TK)
    own = _own_block(qi)
    n_iota = lax.broadcasted_iota(jnp.int32, (8, TQ), 0)
    cnt = jnp.zeros((8, TQ), F32)
    for n2 in range(first + TQ // TK - 1):
        g2 = jnp.broadcast_to(gate[n2:n2 + 1, :], (8, TQ))
        beats = (g2 > gate) | ((g2 == gate) & (n_iota > n2))
        if n2 >= first:
            beats = beats & (own > n2)
        cnt = cnt + jnp.where(beats, 1.0, 0.0)
    return jnp.where(cnt < MOBA_TOPK, 0.0, SEL_NEG)


def _attn_ab_kernel(*refs, moba, nq, sub_max):
    if moba:
        q_ref, k_ref, vt_ref, g_ref, km_ref, o_ref = refs
    else:
        q_ref, k_ref, vt_ref, g_ref, o_ref = refs
    causal = _causal_mask_t()
    km16 = {}
    if moba:
        lane8 = lax.broadcasted_iota(jnp.int32, (8, LANES), 1)
        for u in range(PAIRS):
            for j in (0, 1):
                km = km_ref[0, :, u * LANES:(u + 1) * LANES]
                km = jnp.where((lane8 >= HALF) if j == 1 else (lane8 < HALF), km, 0.0)
                hi = km.astype(BF16)
                lo = (km - hi.astype(F32)).astype(BF16)
                km16[u, j] = jnp.concatenate([hi, lo], axis=0)
    sub = TQ // TK
    for qi in range(nq):
        r0, n = qi * TQ, (qi + 1) * TQ
        first = qi * sub
        for u in range(PAIRS):
            outs = []
            for j in (0, 1):
                c0 = (2 * u + j) * LANES
                qj = q_ref[0, r0:r0 + TQ, c0:c0 + LANES]
                kk = k_ref[0, :n, c0:c0 + LANES]
                if moba and first + sub - 1 > MOBA_TOPK:
                    st = _nt(jnp.concatenate([km16[u, j], kk], axis=0), qj)
                    bias = _moba_bias(st[:8] + st[8:16], qi)
                    st = st[16:]
                    blocks = [st[nb * TK:(nb + 1) * TK] + bias[nb:nb + 1, :] for nb in range(first)]
                    own = _own_block(qi)
                    for r in range(sub):
                        blk = st[(first + r) * TK:(first + r + 1) * TK]
                        if r < sub - 1:
                            blk = blk + jnp.where(own > first + r, bias[first + r:first + r + 1, :], 0.0)
                        blocks.append(blk)
                    st = jnp.concatenate(blocks, axis=0)
                else:
                    st = _nt(kk, qj)
                tail = jnp.where(causal, st[n - TQ:], MASK_NEG)
                st = tail if qi == 0 else jnp.concatenate([st[:n - TQ], tail], axis=0)
                v0 = (2 * u + j) * HALF
                ot = _attend(st, vt_ref[v0:v0 + HALF, :n], sub_max)
                outs.append(_normalized(ot, HALF))
            o = jnp.concatenate(outs, axis=0).T
            gate = _silu(g_ref[0, r0:r0 + TQ, u * LANES:(u + 1) * LANES].astype(F32))
            o_ref[0, r0:r0 + TQ, u * LANES:(u + 1) * LANES] = (o * gate).astype(BF16)


def _step_spec(s, width):
    return pl.BlockSpec((1, s, PAIRS * width), lambda bi, p: (bi, 0, p))


def _vt_spec(s, row0):
    rows = PAIRS * LANES
    return pl.BlockSpec((rows, s), lambda bi, p: (row0 // rows + p, bi))


def _attn_ab(q, k, vt, g, km, *, moba, sub_max):
    b, s, _ = q.shape
    in_specs = [_step_spec(s, 2 * LANES), _step_spec(s, 2 * LANES), _vt_spec(s, 0 if moba else 4 * LANES),
                _step_spec(s, LANES)]
    args = [q, k, vt, g]
    if moba:
        in_specs.append(pl.BlockSpec((1, 8, PAIRS * LANES), lambda bi, p: (bi, 0, p)))
        args.append(km)
    return pl.pallas_call(
        functools.partial(_attn_ab_kernel, moba=moba, nq=s // TQ, sub_max=sub_max),
        grid=(b, 4 // PAIRS),
        in_specs=in_specs,
        out_specs=_step_spec(s, LANES),
        out_shape=jax.ShapeDtypeStruct((b, s, 4 * LANES), BF16),
        compiler_params=_cparams(("arbitrary", "arbitrary")),
        name=("attn_moba" if moba else "attn_fox") + ("_submax" if sub_max else ""),
    )(*args)


def _attn_diff_kernel(q_ref, k_ref, vt_ref, g_ref, lam_ref, sg_ref, o_ref, *, lam_init, nq, sub_max):
    lp = lam_ref[...]
    s1 = jnp.sum(lp[0:1] * lp[1:2], axis=-1, keepdims=True)
    s2 = jnp.sum(lp[2:3] * lp[3:4], axis=-1, keepdims=True)
    lam = jnp.exp(s1) - jnp.exp(s2) + lam_init
    causal = _causal_mask_t()
    for qi in range(nq):
        r0, n = qi * TQ, (qi + 1) * TQ
        for u in range(PAIRS):
            outs = []
            for j in (0, 1):
                c0 = (2 * u + j) * LANES
                st = _nt(k_ref[0, :n, c0:c0 + LANES], q_ref[0, r0:r0 + TQ, c0:c0 + LANES])
                tail = jnp.where(causal, st[n - TQ:], MASK_NEG)
                st = tail if qi == 0 else jnp.concatenate([st[:n - TQ], tail], axis=0)
                ot = _attend(st, vt_ref[u * LANES:(u + 1) * LANES, :n], sub_max)
                outs.append(_normalized(ot, LANES))
            oc = (outs[0] - lam * outs[1]).T
            oc = _rmsnorm_rows(oc, sg_ref[...]) * (1.0 - lam_init)
            gate = _silu(g_ref[0, r0:r0 + TQ, u * LANES:(u + 1) * LANES].astype(F32))
            o_ref[0, r0:r0 + TQ, u * LANES:(u + 1) * LANES] = (oc * gate).astype(BF16)


def _attn_diff(q, k, vt, g, lam, sg, *, lam_init, sub_max):
    b, s, _ = q.shape
    return pl.pallas_call(
        functools.partial(_attn_diff_kernel, lam_init=lam_init, nq=s // TQ, sub_max=sub_max),
        grid=(b, 4 // PAIRS),
        in_specs=[_step_spec(s, 2 * LANES), _step_spec(s, 2 * LANES), _vt_spec(s, 0),
                  _step_spec(s, LANES),
                  pl.BlockSpec(lam.shape, lambda bi, p: (0, 0)),
                  pl.BlockSpec(sg.shape, lambda bi, p: (0, 0))],
        out_specs=_step_spec(s, LANES),
        out_shape=jax.ShapeDtypeStruct((b, s, 4 * LANES), BF16),
        compiler_params=_cparams(("arbitrary", "arbitrary")),
        name="attn_diff" + ("_submax" if sub_max else ""),
    )(q, k, vt, g, lam, sg)


def _attn_swa_kernel(q_ref, k_ref, vt_ref, g_ref, sink_ref, o_ref, *, nq, sub_max):
    masks = {}
    for qi in range(nq):
        t0 = qi * TQ
        start = max(t0 - SWA_BLOCK, 0)
        nkeys = t0 + TQ - start
        if (nkeys, t0 - start) not in masks:
            key = lax.broadcasted_iota(jnp.int32, (nkeys, TQ), 0)
            qry = lax.broadcasted_iota(jnp.int32, (nkeys, TQ), 1)
            rel = qry - key + (t0 - start)
            masks[(nkeys, t0 - start)] = (rel >= 0) & (rel < SWA_BLOCK)
        ok = masks[(nkeys, t0 - start)]
        for u in range(PAIRS):
            outs = []
            for j in (0, 1):
                c0 = (2 * u + j) * LANES
                st = _nt(k_ref[0, start:start + nkeys, j * LANES:(j + 1) * LANES],
                         q_ref[0, t0:t0 + TQ, c0:c0 + LANES])
                st = jnp.where(ok, st, MASK_NEG)
                sink = sink_ref[u, j:j + 1, :]
                if sub_max:
                    m = jnp.maximum(jnp.max(st, axis=0, keepdims=True), sink)
                    st, sink = st - m, sink - m
                ot = _attend(st, vt_ref[j * HALF:(j + 1) * HALF, start:start + nkeys], False)
                outs.append(ot[:HALF] / (ot[HALF:HALF + 1] + jnp.exp(sink)))
            o = jnp.concatenate(outs, axis=0).T
            gate = _silu(g_ref[0, t0:t0 + TQ, u * LANES:(u + 1) * LANES].astype(F32))
            o_ref[0, t0:t0 + TQ, u * LANES:(u + 1) * LANES] = (o * gate).astype(BF16)


def _attn_swa(q, k, vt, g, sinks, *, sub_max):
    b, s, _ = q.shape
    return pl.pallas_call(
        functools.partial(_attn_swa_kernel, nq=s // TQ, sub_max=sub_max),
        grid=(b, 4 // PAIRS),
        in_specs=[_step_spec(s, 2 * LANES), pl.BlockSpec((1, s, 2 * LANES), lambda bi, p: (bi, 0, 0)),
                  pl.BlockSpec((LANES, s), lambda bi, p: (4, bi)), _step_spec(s, LANES),
                  pl.BlockSpec((PAIRS, 2, TQ), lambda bi, p: (p, 0, 0))],
        out_specs=_step_spec(s, LANES),
        out_shape=jax.ShapeDtypeStruct((b, s, 4 * LANES), BF16),
        compiler_params=_cparams(("arbitrary", "arbitrary")),
        name="attn_swa" + ("_submax" if sub_max else ""),
    )(q, k, vt, g, sinks)


SWA_HEAD_ORDER = (0, 4, 1, 5, 2, 6, 3, 7)


def _tile_gain(g, reps):
    return jnp.tile(g.astype(F32), reps)[None, :]


def _logit_bound(gq, gk):
    return HEAD_DIM * SCALE * jnp.max(jnp.abs(gq)) * jnp.max(jnp.abs(gk))


def _guarded(bound, fn, *args):
    return lax.cond(bound <= LOGIT_SAFE, functools.partial(fn, sub_max=False),
                    functools.partial(fn, sub_max=True), *args)


def kernel(x, norm_gain, w_in_ab, b_forget, moba_q_gain, moba_k_gain, fox_q_gain, fox_k_gain, w_out_ab,
           w_in_cd, diff_q_gain, diff_k_gain, diff_lambda, diff_subln_gain, swa_q_gain, swa_k_gain,
           swa_sinks, w_out_cd):
    b, s, d = x.shape
    assert d == 1024 and s % TM == 0 and s // MOBA_BLOCK == 8
    depth = norm_gain.shape[0]
    x2 = x.reshape(b * s, d)

    grp = np.arange(256) // HEAD_DIM
    e256 = jnp.asarray(grp[:, None] == grp[None, :], dtype=BF16)
    tri = jnp.asarray(np.tril(np.ones((256, 256), np.float32)), dtype=BF16)
    perm = np.concatenate([np.arange(HEAD_DIM) + HEAD_DIM * hh for hh in SWA_HEAD_ORDER])
    r3 = lambda a: a.reshape(b, s, a.shape[1])

    res = None
    for layer in range(depth):
        j = layer // 2
        ng = norm_gain[layer][None, :]
        if layer % 2 == 0:
            wj = w_in_ab[j]
            w = jnp.concatenate([wj[:, 0:1024], wj[:, 1536:2048], wj[:, 2048:3072], wj[:, 3584:4096],
                                 jnp.pad(wj[:, 4096:], ((0, 0), (0, LANES - (wj.shape[1] - 4096))))],
                                axis=1).astype(BF16)
            wvt = jnp.concatenate([wj[:, 1024:1536], wj[:, 3072:3584]], axis=1).T.astype(BF16)
            bf = jnp.pad(b_forget[j].astype(F32), (0, LANES - b_forget.shape[1]))[None, :]
            outs = _in_ab(
                x2, res, ng, w, wvt, _tile_gain(moba_q_gain[j], 8), _tile_gain(moba_k_gain[j], 8),
                _tile_gain(fox_q_gain[j], 8), _tile_gain(fox_k_gain[j], 8), bf, e256, tri, seq=s)
            if res is not None:
                x2, outs = outs[0], outs[1:]
            qa, ka, ga, qb, kb, gb, vt, km = outs
            ya = _guarded(_logit_bound(moba_q_gain[j], moba_k_gain[j]), functools.partial(_attn_ab, moba=True),
                          r3(qa), r3(ka), vt, r3(ga), km.reshape(b, 8, 512))
            yb = _guarded(_logit_bound(fox_q_gain[j], fox_k_gain[j]), functools.partial(_attn_ab, moba=False),
                          r3(qb), r3(kb), vt, r3(gb), None)
            res = (ya.reshape(b * s, 512), yb.reshape(b * s, 512), w_out_ab[j].astype(BF16))
        else:
            wj = w_in_cd[j]
            w = jnp.concatenate([wj[:, 0:1024], wj[:, 1536:2048], wj[:, 2048:2560][:, perm],
                                 wj[:, 2560:2688], wj[:, 2816:3328][:, perm]], axis=1).astype(BF16)
            wvt = jnp.concatenate([wj[:, 1024:1536], wj[:, 2688:2816]], axis=1).T.astype(BF16)
            outs = _in_cd(
                x2, res, ng, w, wvt, _tile_gain(diff_q_gain[j], 8), _tile_gain(diff_k_gain[j], 8),
                _tile_gain(swa_q_gain[j], 8), _tile_gain(swa_k_gain[j], 2), e256, seq=s)
            if res is not None:
                x2, outs = outs[0], outs[1:]
            qc, kc, gc, qd, kd, gd, vt = outs
            lam_init = 0.8 - 0.6 * math.exp(-0.3 * layer)
            yc = _guarded(_logit_bound(diff_q_gain[j], diff_k_gain[j]),
                          functools.partial(_attn_diff, lam_init=lam_init),
                          r3(qc), r3(kc), vt, r3(gc), diff_lambda[j].astype(F32),
                          diff_subln_gain[j].astype(F32)[None, :])
            sinks = swa_sinks[j].astype(F32)[np.asarray(SWA_HEAD_ORDER)].reshape(4, 2, 1)
            swa_bound = jnp.maximum(_logit_bound(swa_q_gain[j], swa_k_gain[j]), jnp.max(sinks))
            yd = _guarded(swa_bound, _attn_swa,
                          r3(qd), r3(kd), vt, r3(gd), jnp.broadcast_to(sinks, (4, 2, TQ)))
            wo = w_out_cd[j]
            wo = jnp.concatenate([wo[:512], wo[512:][perm]], axis=0).astype(BF16)
            res = (yc.reshape(b * s, 512), yd.reshape(b * s, 512), wo)
    return _out_proj(x2, *res).reshape(b, s, d)
```

```python
import functools
import math

import numpy as np
import jax
import jax.numpy as jnp
from jax import lax
from jax.experimental import pallas as pl
from jax.experimental.pallas import tpu as pltpu

F32 = jnp.float32
BF16 = jnp.bfloat16

LANES = 128
BF16_ROWS = 16
HEAD_DIM = 64
HALF = 64
EPS = 1e-6
SCALE = HEAD_DIM ** -0.5
TM = 512
TQ = 512
TK = 256
PAIRS = 2
TS = 256
SWA_BLOCK = 128
MOBA_BLOCK = 256
MOBA_TOPK = 3
MASK_NEG = -1e30
SEL_NEG = -1e9
LOGIT_SAFE = 32.0
VMEM_LIMIT = 56 * 1024 * 1024

F_ALIBI = 0
F_FOX = 0


def _alibi_slopes(n):
    s = [2.0 ** (-8.0 * (i + 1) / n) for i in range(n)]
    for v in s:
        m, _ = math.frexp(v)
        assert m == 0.5, "feature-lane ALiBi needs power-of-two slopes"
    return s


def _cparams(sem):
    return pltpu.CompilerParams(dimension_semantics=sem, vmem_limit_bytes=VMEM_LIMIT)


def _nt(a, b):
    return lax.dot_general(a, b, (((1,), (1,)), ((), ())), preferred_element_type=F32)


def _group_sums(sq, e_ref):
    width = sq.shape[1]
    if width == LANES:
        return jnp.dot(sq.astype(BF16), e_ref[:LANES, :LANES], preferred_element_type=F32)
    e = e_ref[...]
    outs = [jnp.dot(sq[:, c:c + 256].astype(BF16), e, preferred_element_type=F32)
            for c in range(0, width, 256)]
    return outs[0] if len(outs) == 1 else jnp.concatenate(outs, axis=1)


def _head_norm(p, gain_row, e_ref):
    ssq = _group_sums(p * p, e_ref)
    return p * lax.rsqrt(ssq * (1.0 / HEAD_DIM) + EPS) * gain_row


def _lane_iota(rows):
    return lax.broadcasted_iota(jnp.int32, (rows, LANES), 1)


def _pos_features(i, tiles_per_seq):
    lane = _lane_iota(TM)
    pos = (i % tiles_per_seq) * TM + lax.broadcasted_iota(jnp.int32, (TM, LANES), 0)
    a64 = ((pos >> 6) << 6).astype(F32)
    a = (pos >> 6).astype(F32)
    b = (pos & 63).astype(F32)
    out = []
    for j in (0, 1):
        base = (1 - j) * HALF + F_ALIBI
        qb = jnp.where(lane == base, -a64,
             jnp.where(lane == base + 1, -b,
             jnp.where(lane == base + 2, 64.0,
             jnp.where(lane == base + 3, 1.0, 0.0))))
        kf = jnp.where((lane == base) | (lane == base + 1), 1.0,
             jnp.where(lane == base + 2, a,
             jnp.where(lane == base + 3, b, 0.0)))
        out.append((qb, kf))
    return lane, out


def _rmsnorm_rows(x, gain_row):
    return x * lax.rsqrt(jnp.mean(x * x, axis=-1, keepdims=True) + EPS) * gain_row


def _residual_tile(refs, fused, n_in):
    if not fused:
        return refs[0][...], refs[1:]
    x_ref, ya_ref, yb_ref, wo_ref = refs[:4]
    rest = refs[4:]
    half = ya_ref.shape[1]
    x = (x_ref[...]
         + jnp.dot(ya_ref[...], wo_ref[:half, :], preferred_element_type=F32)
         + jnp.dot(yb_ref[...], wo_ref[half:, :], preferred_element_type=F32))
    rest[n_in][...] = x
    return x, rest[:n_in] + rest[n_in + 1:]


def _residual_specs(x2, res):
    row = lambda width: pl.BlockSpec((TM, width), lambda i: (i, 0))
    if res is None:
        return [x2], [row(x2.shape[1])], [], []
    ya, yb, wo = res
    return ([x2, ya, yb, wo],
            [row(x2.shape[1]), row(ya.shape[1]), row(yb.shape[1]), pl.BlockSpec(wo.shape, lambda i: (0, 0))],
            [row(x2.shape[1])], [jax.ShapeDtypeStruct(x2.shape, F32)])


def _store_blocks(dst_ref, src, feats):
    lane = _lane_iota(src.shape[0])
    for p in range(src.shape[1] // LANES):
        pair = src[:, p * LANES:(p + 1) * LANES]
        for j in (0, 1):
            data = (lane >= HALF) if j == 1 else (lane < HALF)
            blk = jnp.where(data, pair, feats(p, j))
            dst_ref[:, (2 * p + j) * LANES:(2 * p + j + 1) * LANES] = blk.astype(BF16)


def _in_ab_kernel(*refs, tiles_per_seq, fused):
    x, refs = _residual_tile(refs, fused, 10)
    (ng_ref, w_ref, wvt_ref, gqa_ref, gka_ref, gqb_ref, gkb_ref, bf_ref, e_ref, tri_ref,
     qa_ref, ka_ref, ga_ref, qb_ref, kb_ref, gb_ref, vt_ref, km_ref, carry_ref) = refs
    i = pl.program_id(0)

    @pl.when(i % tiles_per_seq == 0)
    def _():
        carry_ref[...] = jnp.zeros_like(carry_ref)

    h = _rmsnorm_rows(x, ng_ref[...]).astype(BF16)

    def proj(c0, width):
        return jnp.dot(h, w_ref[:, c0:c0 + width], preferred_element_type=F32)

    lane, pf = _pos_features(i, tiles_per_seq)
    slopes = _alibi_slopes(8)
    vt_ref[...] = _nt(wvt_ref[...], h).astype(BF16)

    qa = _head_norm(proj(0, 512), gqa_ref[...] * SCALE, e_ref)
    _store_blocks(qa_ref, qa, lambda p, j: pf[j][0] * slopes[2 * p + j])
    ka = _head_norm(proj(512, 512), gka_ref[...], e_ref)
    _store_blocks(ka_ref, ka, lambda p, j: pf[j][1])
    km = [jnp.mean(ka[r * MOBA_BLOCK:(r + 1) * MOBA_BLOCK], axis=0, keepdims=True)
          for r in range(TM // MOBA_BLOCK)]
    km_ref[0] = jnp.concatenate(km, axis=0)
    ga_ref[...] = proj(1024, 512).astype(BF16)

    z = proj(3072, LANES) + bf_ref[...]
    logf = jnp.minimum(z, 0.0) - jnp.log1p(jnp.exp(-jnp.abs(z)))
    carry = carry_ref[...]
    tri = tri_ref[...]
    parts = []
    for r in range(TM // 256):
        lf = logf[r * 256:(r + 1) * 256]
        hi = lf.astype(BF16)
        r1 = lf - hi.astype(F32)
        mid = r1.astype(BF16)
        lo = (r1 - mid.astype(F32)).astype(BF16)
        cs = (jnp.dot(tri, hi, preferred_element_type=F32)
              + jnp.dot(tri, mid, preferred_element_type=F32)
              + jnp.dot(tri, lo, preferred_element_type=F32)) + carry
        carry = cs[255:256, :]
        parts.append(cs)
    carry_ref[...] = carry
    c = jnp.concatenate(parts, axis=0)
    chi = c.astype(BF16).astype(F32)
    r1 = c - chi
    cmid = r1.astype(BF16).astype(F32)
    clo = r1 - cmid

    def bcast(arr, hh):
        return jnp.broadcast_to(arr[:, hh:hh + 1], (TM, LANES))

    def fox_q(p, j):
        hh, base = 2 * p + j, (1 - j) * HALF + F_FOX
        ones = jnp.where((lane >= base + 3) & (lane < base + 6), 1.0, 0.0)
        return jnp.where(lane == base, bcast(chi, hh),
               jnp.where(lane == base + 1, bcast(cmid, hh),
               jnp.where(lane == base + 2, bcast(clo, hh), ones)))

    def fox_k(p, j):
        hh, base = 2 * p + j, (1 - j) * HALF + F_FOX
        ones = jnp.where((lane >= base) & (lane < base + 3), 1.0, 0.0)
        return jnp.where(lane == base + 3, -bcast(chi, hh),
               jnp.where(lane == base + 4, -bcast(cmid, hh),
               jnp.where(lane == base + 5, -bcast(clo, hh), ones)))

    qb = _head_norm(proj(1536, 512), gqb_ref[...] * SCALE, e_ref)
    _store_blocks(qb_ref, qb, fox_q)
    kb = _head_norm(proj(2048, 512), gkb_ref[...], e_ref)
    _store_blocks(kb_ref, kb, fox_k)
    gb_ref[...] = proj(2560, 512).astype(BF16)


def _in_ab(x2, res, ng, w, wvt, gqa, gka, gqb, gkb, bf, e256, tri, *, seq):
    n = x2.shape[0]
    nt = n // TM
    row = lambda width: pl.BlockSpec((TM, width), lambda i: (i, 0))
    full = lambda a: pl.BlockSpec(a.shape, lambda i: (0,) * a.ndim)
    bf16 = lambda width: jax.ShapeDtypeStruct((n, width), BF16)
    res_in, res_in_specs, res_out_specs, res_out_shapes = _residual_specs(x2, res)
    consts = [ng, w, wvt, gqa, gka, gqb, gkb, bf, e256, tri]
    return pl.pallas_call(
        functools.partial(_in_ab_kernel, tiles_per_seq=seq // TM, fused=res is not None),
        grid=(nt,),
        in_specs=res_in_specs + [full(a) for a in consts],
        out_specs=res_out_specs + [row(1024), row(1024), row(512), row(1024), row(1024), row(512),
                                   pl.BlockSpec((wvt.shape[0], TM), lambda i: (0, i)),
                                   pl.BlockSpec((1, TM // MOBA_BLOCK, 512), lambda i: (i, 0, 0))],
        out_shape=res_out_shapes + [bf16(1024), bf16(1024), bf16(512), bf16(1024), bf16(1024), bf16(512),
                                    jax.ShapeDtypeStruct((wvt.shape[0], n), BF16),
                                    jax.ShapeDtypeStruct((nt, TM // MOBA_BLOCK, 512), F32)],
        scratch_shapes=[pltpu.VMEM((1, LANES), F32)],
        compiler_params=_cparams(("arbitrary",)),
        name="in_proj_ab",
    )(*res_in, *consts)


def _in_cd_kernel(*refs, tiles_per_seq, fused):
    x, refs = _residual_tile(refs, fused, 8)
    (ng_ref, w_ref, wvt_ref, gqc_ref, gkc_ref, gqd_ref, gkd_ref, e_ref,
     qc_ref, kc_ref, gc_ref, qd_ref, kd_ref, gd_ref, vt_ref) = refs
    i = pl.program_id(0)
    h = _rmsnorm_rows(x, ng_ref[...]).astype(BF16)

    def proj(c0, width):
        return jnp.dot(h, w_ref[:, c0:c0 + width], preferred_element_type=F32)

    _, pf = _pos_features(i, tiles_per_seq)
    dslopes = _alibi_slopes(4)
    sslopes = _alibi_slopes(8)
    vt_ref[...] = _nt(wvt_ref[...], h).astype(BF16)

    qc = _head_norm(proj(0, 512), gqc_ref[...] * SCALE, e_ref)
    _store_blocks(qc_ref, qc, lambda p, j: pf[j][0] * dslopes[p])
    kc = _head_norm(proj(512, 512), gkc_ref[...], e_ref)
    _store_blocks(kc_ref, kc, lambda p, j: pf[j][1])
    gc_ref[...] = proj(1024, 512).astype(BF16)

    qd = _head_norm(proj(1536, 512), gqd_ref[...] * SCALE, e_ref)
    _store_blocks(qd_ref, qd, lambda p, j: pf[j][0] * sslopes[p + 4 * j])
    kd = _head_norm(proj(2048, LANES), gkd_ref[...], e_ref)
    _store_blocks(kd_ref, kd, lambda p, j: pf[j][1])
    gd_ref[...] = proj(2176, 512).astype(BF16)


def _in_cd(x2, res, ng, w, wvt, gqc, gkc, gqd, gkd, e256, *, seq):
    n = x2.shape[0]
    nt = n // TM
    row = lambda width: pl.BlockSpec((TM, width), lambda i: (i, 0))
    full = lambda a: pl.BlockSpec(a.shape, lambda i: (0,) * a.ndim)
    bf16 = lambda width: jax.ShapeDtypeStruct((n, width), BF16)
    res_in, res_in_specs, res_out_specs, res_out_shapes = _residual_specs(x2, res)
    consts = [ng, w, wvt, gqc, gkc, gqd, gkd, e256]
    return pl.pallas_call(
        functools.partial(_in_cd_kernel, tiles_per_seq=seq // TM, fused=res is not None),
        grid=(nt,),
        in_specs=res_in_specs + [full(a) for a in consts],
        out_specs=res_out_specs + [row(1024), row(1024), row(512), row(1024), row(256), row(512),
                                   pl.BlockSpec((wvt.shape[0], TM), lambda i: (0, i))],
        out_shape=res_out_shapes + [bf16(1024), bf16(1024), bf16(512), bf16(1024), bf16(256), bf16(512),
                                    jax.ShapeDtypeStruct((wvt.shape[0], n), BF16)],
        compiler_params=_cparams(("arbitrary",)),
        name="in_proj_cd",
    )(*res_in, *consts)


def _out_kernel(x_ref, ya_ref, yb_ref, w_ref, o_ref):
    half = ya_ref.shape[1]
    o_ref[...] = (x_ref[...]
                  + jnp.dot(ya_ref[...], w_ref[:half, :], preferred_element_type=F32)
                  + jnp.dot(yb_ref[...], w_ref[half:, :], preferred_element_type=F32))


def _out_proj(x2, ya, yb, w):
    n, d = x2.shape
    row = lambda width: pl.BlockSpec((TM, width), lambda i: (i, 0))
    return pl.pallas_call(
        _out_kernel,
        grid=(n // TM,),
        in_specs=[row(d), row(ya.shape[1]), row(yb.shape[1]), pl.BlockSpec(w.shape, lambda i: (0, 0))],
        out_specs=row(d),
        out_shape=jax.ShapeDtypeStruct((n, d), F32),
        compiler_params=_cparams(("arbitrary",)),
        name="out_proj",
    )(x2, ya, yb, w)


def _silu(g):
    return g * (1.0 / (1.0 + jnp.exp(-g)))


def _attend(pt, vt):
    lhs = jnp.concatenate([vt, jnp.ones((BF16_ROWS, vt.shape[1]), BF16)], axis=0)
    return jnp.dot(lhs, pt, preferred_element_type=F32)


def _causal_probs(st, causal, sub_max):
    n = st.shape[0]
    if sub_max:
        tail = jnp.where(causal, st[n - TQ:], MASK_NEG)
        st = tail if n == TQ else jnp.concatenate([st[:n - TQ], tail], axis=0)
        return jnp.exp(st - jnp.max(st, axis=0, keepdims=True)).astype(BF16)
    parts = [jnp.exp(st[:n - TQ]).astype(BF16)] if n > TQ else []
    for k0 in range(0, TQ, TK):
        rows = st[n - TQ + k0:n - TQ + k0 + TK]
        live = jnp.exp(jnp.where(causal[k0:k0 + TK, k0:], rows[:, k0:], MASK_NEG)).astype(BF16)
        parts.append(live if k0 == 0 else jnp.concatenate([jnp.zeros((TK, k0), BF16), live], axis=1))
    return jnp.concatenate(parts, axis=0)


def _normalized(ot, dims):
    return ot[:dims] / ot[dims:dims + 1]


def _causal_mask_t():
    key = lax.broadcasted_iota(jnp.int32, (TQ, TQ), 0)
    qry = lax.broadcasted_iota(jnp.int32, (TQ, TQ), 1)
    return key <= qry


def _own_block(qi):
    return qi * (TQ // TK) + lax.broadcasted_iota(jnp.int32, (1, TQ), 1) // TK


def _moba_bias(gate, qi):
    first = qi * (TQ // TK)
    own = _own_block(qi)
    n_iota = lax.broadcasted_iota(jnp.int32, (8, TQ), 0)
    cnt = jnp.zeros((8, TQ), F32)
    for n2 in range(first + TQ // TK - 1):
        g2 = jnp.broadcast_to(gate[n2:n2 + 1, :], (8, TQ))
        beats = (g2 > gate) | ((g2 == gate) & (n_iota > n2))
        if n2 >= first:
            beats = beats & (own > n2)
        cnt = cnt + jnp.where(beats, 1.0, 0.0)
    return jnp.where(cnt < MOBA_TOPK, 0.0, SEL_NEG)


def _attn_ab_kernel(*refs, moba, nq, sub_max):
    if moba:
        q_ref, k_ref, vt_ref, g_ref, km_ref, o_ref = refs
    else:
        q_ref, k_ref, vt_ref, g_ref, o_ref = refs
    causal = _causal_mask_t()
    km16 = {}
    if moba:
        lane8 = lax.broadcasted_iota(jnp.int32, (8, LANES), 1)
        for u in range(PAIRS):
            for j in (0, 1):
                km = km_ref[0, :, u * LANES:(u + 1) * LANES]
                km = jnp.where((lane8 >= HALF) if j == 1 else (lane8 < HALF), km, 0.0)
                hi = km.astype(BF16)
                lo = (km - hi.astype(F32)).astype(BF16)
                km16[u, j] = jnp.concatenate([hi, lo], axis=0)
    sub = TQ // TK
    for qi in range(nq):
        r0, n = qi * TQ, (qi + 1) * TQ
        first = qi * sub
        for u in range(PAIRS):
            outs = []
            for j in (0, 1):
                c0 = (2 * u + j) * LANES
                qj = q_ref[0, r0:r0 + TQ, c0:c0 + LANES]
                kk = k_ref[0, :n, c0:c0 + LANES]
                if moba and first + sub - 1 > MOBA_TOPK:
                    st = _nt(jnp.concatenate([km16[u, j], kk], axis=0), qj)
                    bias = _moba_bias(st[:8] + st[8:16], qi)
                    st = st[16:]
                    blocks = [st[nb * TK:(nb + 1) * TK] + bias[nb:nb + 1, :] for nb in range(first)]
                    own = _own_block(qi)
                    for r in range(sub):
                        blk = st[(first + r) * TK:(first + r + 1) * TK]
                        if r < sub - 1:
                            blk = blk + jnp.where(own > first + r, bias[first + r:first + r + 1, :], 0.0)
                        blocks.append(blk)
                    st = jnp.concatenate(blocks, axis=0)
                else:
                    st = _nt(kk, qj)
                v0 = (2 * u + j) * HALF
                ot = _attend(_causal_probs(st, causal, sub_max), vt_ref[v0:v0 + HALF, :n])
                outs.append(_normalized(ot, HALF))
            o = jnp.concatenate(outs, axis=0).T
            gate = _silu(g_ref[0, r0:r0 + TQ, u * LANES:(u + 1) * LANES].astype(F32))
            o_ref[0, r0:r0 + TQ, u * LANES:(u + 1) * LANES] = (o * gate).astype(BF16)


def _step_spec(s, width):
    return pl.BlockSpec((1, s, PAIRS * width), lambda bi, p: (bi, 0, p))


def _vt_spec(s, row0):
    rows = PAIRS * LANES
    return pl.BlockSpec((rows, s), lambda bi, p: (row0 // rows + p, bi))


def _attn_ab(q, k, vt, g, km, *, moba, sub_max):
    b, s, _ = q.shape
    in_specs = [_step_spec(s, 2 * LANES), _step_spec(s, 2 * LANES), _vt_spec(s, 0 if moba else 4 * LANES),
                _step_spec(s, LANES)]
    args = [q, k, vt, g]
    if moba:
        in_specs.append(pl.BlockSpec((1, 8, PAIRS * LANES), lambda bi, p: (bi, 0, p)))
        args.append(km)
    return pl.pallas_call(
        functools.partial(_attn_ab_kernel, moba=moba, nq=s // TQ, sub_max=sub_max),
        grid=(b, 4 // PAIRS),
        in_specs=in_specs,
        out_specs=_step_spec(s, LANES),
        out_shape=jax.ShapeDtypeStruct((b, s, 4 * LANES), BF16),
        compiler_params=_cparams(("arbitrary", "arbitrary")),
        name=("attn_moba" if moba else "attn_fox") + ("_submax" if sub_max else ""),
    )(*args)


def _attn_diff_kernel(q_ref, k_ref, vt_ref, g_ref, lam_ref, sg_ref, o_ref, *, lam_init, nq, sub_max):
    lp = lam_ref[...]
    s1 = jnp.sum(lp[0:1] * lp[1:2], axis=-1, keepdims=True)
    s2 = jnp.sum(lp[2:3] * lp[3:4], axis=-1, keepdims=True)
    lam = jnp.exp(s1) - jnp.exp(s2) + lam_init
    causal = _causal_mask_t()
    for qi in range(nq):
        r0, n = qi * TQ, (qi + 1) * TQ
        for u in range(PAIRS):
            outs = []
            for j in (0, 1):
                c0 = (2 * u + j) * LANES
                st = _nt(k_ref[0, :n, c0:c0 + LANES], q_ref[0, r0:r0 + TQ, c0:c0 + LANES])
                ot = _attend(_causal_probs(st, causal, sub_max),
                             vt_ref[u * LANES:(u + 1) * LANES, :n])
                outs.append(_normalized(ot, LANES))
            oc = (outs[0] - lam * outs[1]).T
            oc = _rmsnorm_rows(oc, sg_ref[...]) * (1.0 - lam_init)
            gate = _silu(g_ref[0, r0:r0 + TQ, u * LANES:(u + 1) * LANES].astype(F32))
            o_ref[0, r0:r0 + TQ, u * LANES:(u + 1) * LANES] = (oc * gate).astype(BF16)


def _attn_diff(q, k, vt, g, lam, sg, *, lam_init, sub_max):
    b, s, _ = q.shape
    return pl.pallas_call(
        functools.partial(_attn_diff_kernel, lam_init=lam_init, nq=s // TQ, sub_max=sub_max),
        grid=(b, 4 // PAIRS),
        in_specs=[_step_spec(s, 2 * LANES), _step_spec(s, 2 * LANES), _vt_spec(s, 0),
                  _step_spec(s, LANES),
                  pl.BlockSpec(lam.shape, lambda bi, p: (0, 0)),
                  pl.BlockSpec(sg.shape, lambda bi, p: (0, 0))],
        out_specs=_step_spec(s, LANES),
        out_shape=jax.ShapeDtypeStruct((b, s, 4 * LANES), BF16),
        compiler_params=_cparams(("arbitrary", "arbitrary")),
        name="attn_diff" + ("_submax" if sub_max else ""),
    )(q, k, vt, g, lam, sg)


def _attn_swa_kernel(q_ref, k_ref, vt_ref, g_ref, sink_ref, o_ref, *, nt, sub_max):
    w = SWA_BLOCK
    key = lax.broadcasted_iota(jnp.int32, (w, w), 0)
    qry = lax.broadcasted_iota(jnp.int32, (w, w), 1)
    bias = {0: jnp.where(key <= qry, 0.0, MASK_NEG),
            1: jnp.where(key > qry, 0.0, MASK_NEG)}
    dark = jnp.full((w, w), MASK_NEG, F32)
    for ti in range(nt):
        t0 = ti * TS
        start = max(t0 - w, 0)
        nkeys = t0 + TS - start
        heads = []
        for j in (0, 1):
            qs = jnp.concatenate([q_ref[0, t0:t0 + TS, (2 * p + j) * LANES:(2 * p + j + 1) * LANES]
                                  for p in range(4)], axis=0)
            st = _nt(k_ref[0, start:start + nkeys, j * LANES:(j + 1) * LANES], qs)
            sink = sink_ref[j:j + 1, :]

            def blocks(fn):
                rows = []
                for r in range(nkeys // w):
                    cols = []
                    for c in range(4 * TS // w):
                        dist = (t0 // w + c % (TS // w)) - (start // w + r)
                        cols.append(fn(st[r * w:(r + 1) * w, c * w:(c + 1) * w], dist))
                    rows.append(jnp.concatenate(cols, axis=1))
                return jnp.concatenate(rows, axis=0)

            if sub_max:
                st = blocks(lambda blk, d: blk + bias.get(d, dark))
                m = jnp.maximum(jnp.max(st, axis=0, keepdims=True), sink)
                pt, sink = jnp.exp(st - m).astype(BF16), sink - m
            else:
                pt = blocks(lambda blk, d: jnp.exp(blk + bias[d]).astype(BF16) if d in bias
                            else jnp.zeros((w, w), BF16))
            ot = _attend(pt, vt_ref[j * HALF:(j + 1) * HALF, start:start + nkeys])
            heads.append(ot[:HALF] / (ot[HALF:HALF + 1] + jnp.exp(sink)))
        for p in range(4):
            o = jnp.concatenate([hd[:, p * TS:(p + 1) * TS] for hd in heads], axis=0).T
            gate = _silu(g_ref[0, t0:t0 + TS, p * LANES:(p + 1) * LANES].astype(F32))
            o_ref[0, t0:t0 + TS, p * LANES:(p + 1) * LANES] = (o * gate).astype(BF16)


def _attn_swa(q, k, vt, g, sinks, *, sub_max):
    b, s, _ = q.shape
    whole = lambda width: pl.BlockSpec((1, s, width), lambda bi: (bi, 0, 0))
    return pl.pallas_call(
        functools.partial(_attn_swa_kernel, nt=s // TS, sub_max=sub_max),
        grid=(b,),
        in_specs=[whole(8 * LANES), whole(2 * LANES), pl.BlockSpec((LANES, s), lambda bi: (4, bi)),
                  whole(4 * LANES), pl.BlockSpec(sinks.shape, lambda bi: (0, 0))],
        out_specs=whole(4 * LANES),
        out_shape=jax.ShapeDtypeStruct((b, s, 4 * LANES), BF16),
        compiler_params=_cparams(("arbitrary",)),
        name="attn_swa" + ("_submax" if sub_max else ""),
    )(q, k, vt, g, sinks)


SWA_HEAD_ORDER = (0, 4, 1, 5, 2, 6, 3, 7)


def _tile_gain(g, reps):
    return jnp.tile(g.astype(F32), reps)[None, :]


def _logit_bound(gq, gk):
    return HEAD_DIM * SCALE * jnp.max(jnp.abs(gq)) * jnp.max(jnp.abs(gk))


def _guarded(bound, fn, *args):
    return lax.cond(bound <= LOGIT_SAFE, functools.partial(fn, sub_max=False),
                    functools.partial(fn, sub_max=True), *args)


def kernel(x, norm_gain, w_in_ab, b_forget, moba_q_gain, moba_k_gain, fox_q_gain, fox_k_gain, w_out_ab,
           w_in_cd, diff_q_gain, diff_k_gain, diff_lambda, diff_subln_gain, swa_q_gain, swa_k_gain,
           swa_sinks, w_out_cd):
    b, s, d = x.shape
    assert d == 1024 and s % TM == 0 and s // MOBA_BLOCK == 8
    depth = norm_gain.shape[0]
    x2 = x.reshape(b * s, d)

    grp = np.arange(256) // HEAD_DIM
    e256 = jnp.asarray(grp[:, None] == grp[None, :], dtype=BF16)
    tri = jnp.asarray(np.tril(np.ones((256, 256), np.float32)), dtype=BF16)
    perm = np.concatenate([np.arange(HEAD_DIM) + HEAD_DIM * hh for hh in SWA_HEAD_ORDER])
    r3 = lambda a: a.reshape(b, s, a.shape[1])

    res = None
    for layer in range(depth):
        j = layer // 2
        ng = norm_gain[layer][None, :]
        if layer % 2 == 0:
            wj = w_in_ab[j]
            w = jnp.concatenate([wj[:, 0:1024], wj[:, 1536:2048], wj[:, 2048:3072], wj[:, 3584:4096],
                                 jnp.pad(wj[:, 4096:], ((0, 0), (0, LANES - (wj.shape[1] - 4096))))],
                                axis=1).astype(BF16)
            wvt = jnp.concatenate([wj[:, 1024:1536], wj[:, 3072:3584]], axis=1).T.astype(BF16)
            bf = jnp.pad(b_forget[j].astype(F32), (0, LANES - b_forget.shape[1]))[None, :]
            outs = _in_ab(
                x2, res, ng, w, wvt, _tile_gain(moba_q_gain[j], 8), _tile_gain(moba_k_gain[j], 8),
                _tile_gain(fox_q_gain[j], 8), _tile_gain(fox_k_gain[j], 8), bf, e256, tri, seq=s)
            if res is not None:
                x2, outs = outs[0], outs[1:]
            qa, ka, ga, qb, kb, gb, vt, km = outs
            ya = _guarded(_logit_bound(moba_q_gain[j], moba_k_gain[j]), functools.partial(_attn_ab, moba=True),
                          r3(qa), r3(ka), vt, r3(ga), km.reshape(b, 8, 512))
            yb = _guarded(_logit_bound(fox_q_gain[j], fox_k_gain[j]), functools.partial(_attn_ab, moba=False),
                          r3(qb), r3(kb), vt, r3(gb), None)
            res = (ya.reshape(b * s, 512), yb.reshape(b * s, 512), w_out_ab[j].astype(BF16))
        else:
            wj = w_in_cd[j]
            w = jnp.concatenate([wj[:, 0:1024], wj[:, 1536:2048], wj[:, 2048:2560][:, perm],
                                 wj[:, 2560:2688], wj[:, 2816:3328][:, perm]], axis=1).astype(BF16)
            wvt = jnp.concatenate([wj[:, 1024:1536], wj[:, 2688:2816]], axis=1).T.astype(BF16)
            outs = _in_cd(
                x2, res, ng, w, wvt, _tile_gain(diff_q_gain[j], 8), _tile_gain(diff_k_gain[j], 8),
                _tile_gain(swa_q_gain[j], 8), _tile_gain(swa_k_gain[j], 2), e256, seq=s)
            if res is not None:
                x2, outs = outs[0], outs[1:]
            qc, kc, gc, qd, kd, gd, vt = outs
            lam_init = 0.8 - 0.6 * math.exp(-0.3 * layer)
            yc = _guarded(_logit_bound(diff_q_gain[j], diff_k_gain[j]),
                          functools.partial(_attn_diff, lam_init=lam_init),
                          r3(qc), r3(kc), vt, r3(gc), diff_lambda[j].astype(F32),
                          diff_subln_gain[j].astype(F32)[None, :])
            sinks = swa_sinks[j].astype(F32).reshape(2, 4)
            swa_bound = jnp.maximum(_logit_bound(swa_q_gain[j], swa_k_gain[j]), jnp.max(sinks))
            yd = _guarded(swa_bound, _attn_swa,
                          r3(qd), r3(kd), vt, r3(gd), jnp.repeat(sinks, TS, axis=1))
            wo = w_out_cd[j]
            wo = jnp.concatenate([wo[:512], wo[512:][perm]], axis=0).astype(BF16)
            res = (yc.reshape(b * s, 512), yd.reshape(b * s, 512), wo)
    return _out_proj(x2, *res).reshape(b, s, d)
```

```python
import functools
import math

import numpy as np
import jax
import jax.numpy as jnp
from jax import lax
from jax.experimental import pallas as pl
from jax.experimental.pallas import tpu as pltpu

F32 = jnp.float32
BF16 = jnp.bfloat16

LANES = 128
BF16_ROWS = 16
HEAD_DIM = 64
HALF = 64
EPS = 1e-6
SCALE = HEAD_DIM ** -0.5
TM = 512
TQ = 512
TK = 256
PAIRS = 2
TS = 256
SWA_BLOCK = 128
MOBA_BLOCK = 256
MOBA_TOPK = 3
MASK_NEG = -1e30
SEL_NEG = -1e9
LOGIT_SAFE = 32.0
VMEM_LIMIT = 56 * 1024 * 1024

F_ALIBI = 0
F_FOX = 0


def _alibi_slopes(n):
    s = [2.0 ** (-8.0 * (i + 1) / n) for i in range(n)]
    for v in s:
        m, _ = math.frexp(v)
        assert m == 0.5, "feature-lane ALiBi needs power-of-two slopes"
    return s


def _cparams(sem):
    return pltpu.CompilerParams(dimension_semantics=sem, vmem_limit_bytes=VMEM_LIMIT)


def _nt(a, b):
    return lax.dot_general(a, b, (((1,), (1,)), ((), ())), preferred_element_type=F32)


def _group_sums(sq, e_ref):
    width = sq.shape[1]
    if width == LANES:
        return jnp.dot(sq.astype(BF16), e_ref[:LANES, :LANES], preferred_element_type=F32)
    e = e_ref[...]
    outs = [jnp.dot(sq[:, c:c + 256].astype(BF16), e, preferred_element_type=F32)
            for c in range(0, width, 256)]
    return outs[0] if len(outs) == 1 else jnp.concatenate(outs, axis=1)


def _head_norm(p, gain_row, e_ref):
    ssq = _group_sums(p * p, e_ref)
    return p * lax.rsqrt(ssq * (1.0 / HEAD_DIM) + EPS) * gain_row


def _lane_iota(rows):
    return lax.broadcasted_iota(jnp.int32, (rows, LANES), 1)


def _pos_features(i, tiles_per_seq):
    lane = _lane_iota(TM)
    pos = (i % tiles_per_seq) * TM + lax.broadcasted_iota(jnp.int32, (TM, LANES), 0)
    a64 = ((pos >> 6) << 6).astype(F32)
    a = (pos >> 6).astype(F32)
    b = (pos & 63).astype(F32)
    out = []
    for j in (0, 1):
        base = (1 - j) * HALF + F_ALIBI
        qb = jnp.where(lane == base, -a64,
             jnp.where(lane == base + 1, -b,
             jnp.where(lane == base + 2, 64.0,
             jnp.where(lane == base + 3, 1.0, 0.0))))
        kf = jnp.where((lane == base) | (lane == base + 1), 1.0,
             jnp.where(lane == base + 2, a,
             jnp.where(lane == base + 3, b, 0.0)))
        out.append((qb, kf))
    return lane, out


def _rmsnorm_rows(x, gain_row):
    return x * lax.rsqrt(jnp.mean(x * x, axis=-1, keepdims=True) + EPS) * gain_row


def _residual_tile(refs, fused, n_in):
    if not fused:
        return refs[0][...], refs[1:]
    x_ref, ya_ref, yb_ref, wo_ref = refs[:4]
    rest = refs[4:]
    half = ya_ref.shape[1]
    x = (x_ref[...]
         + jnp.dot(ya_ref[...], wo_ref[:half, :], preferred_element_type=F32)
         + jnp.dot(yb_ref[...], wo_ref[half:, :], preferred_element_type=F32))
    rest[n_in][...] = x
    return x, rest[:n_in] + rest[n_in + 1:]


def _residual_specs(x2, res):
    row = lambda width: pl.BlockSpec((TM, width), lambda i: (i, 0))
    if res is None:
        return [x2], [row(x2.shape[1])], [], []
    ya, yb, wo = res
    return ([x2, ya, yb, wo],
            [row(x2.shape[1]), row(ya.shape[1]), row(yb.shape[1]), pl.BlockSpec(wo.shape, lambda i: (0, 0))],
            [row(x2.shape[1])], [jax.ShapeDtypeStruct(x2.shape, F32)])


def _store_blocks(dst_ref, src, feats):
    lane = _lane_iota(src.shape[0])
    for p in range(src.shape[1] // LANES):
        pair = src[:, p * LANES:(p + 1) * LANES]
        for j in (0, 1):
            data = (lane >= HALF) if j == 1 else (lane < HALF)
            blk = jnp.where(data, pair, feats(p, j))
            dst_ref[:, (2 * p + j) * LANES:(2 * p + j + 1) * LANES] = blk.astype(BF16)


def _in_ab_kernel(*refs, tiles_per_seq, fused):
    x, refs = _residual_tile(refs, fused, 10)
    (ng_ref, w_ref, wvt_ref, gqa_ref, gka_ref, gqb_ref, gkb_ref, bf_ref, e_ref, tri_ref,
     qa_ref, ka_ref, ga_ref, qb_ref, kb_ref, gb_ref, vt_ref, km_ref, carry_ref) = refs
    i = pl.program_id(0)

    @pl.when(i % tiles_per_seq == 0)
    def _():
        carry_ref[...] = jnp.zeros_like(carry_ref)

    h = _rmsnorm_rows(x, ng_ref[...]).astype(BF16)

    def proj(c0, width):
        return jnp.dot(h, w_ref[:, c0:c0 + width], preferred_element_type=F32)

    lane, pf = _pos_features(i, tiles_per_seq)
    slopes = _alibi_slopes(8)
    vt_ref[...] = _nt(wvt_ref[...], h).astype(BF16)

    qa = _head_norm(proj(0, 512), gqa_ref[...] * SCALE, e_ref)
    _store_blocks(qa_ref, qa, lambda p, j: pf[j][0] * slopes[2 * p + j])
    ka = _head_norm(proj(512, 512), gka_ref[...], e_ref)
    _store_blocks(ka_ref, ka, lambda p, j: pf[j][1])
    km = [jnp.mean(ka[r * MOBA_BLOCK:(r + 1) * MOBA_BLOCK], axis=0, keepdims=True)
          for r in range(TM // MOBA_BLOCK)]
    km_ref[0] = jnp.concatenate(km, axis=0)
    ga_ref[...] = proj(1024, 512).astype(BF16)

    z = proj(3072, LANES) + bf_ref[...]
    logf = jnp.minimum(z, 0.0) - jnp.log1p(jnp.exp(-jnp.abs(z)))
    carry = carry_ref[...]
    tri = tri_ref[...]
    parts = []
    for r in range(TM // 256):
        lf = logf[r * 256:(r + 1) * 256]
        hi = lf.astype(BF16)
        r1 = lf - hi.astype(F32)
        mid = r1.astype(BF16)
        lo = (r1 - mid.astype(F32)).astype(BF16)
        cs = (jnp.dot(tri, hi, preferred_element_type=F32)
              + jnp.dot(tri, mid, preferred_element_type=F32)
              + jnp.dot(tri, lo, preferred_element_type=F32)) + carry
        carry = cs[255:256, :]
        parts.append(cs)
    carry_ref[...] = carry
    c = jnp.concatenate(parts, axis=0)
    chi = c.astype(BF16).astype(F32)
    r1 = c - chi
    cmid = r1.astype(BF16).astype(F32)
    clo = r1 - cmid

    def bcast(arr, hh):
        return jnp.broadcast_to(arr[:, hh:hh + 1], (TM, LANES))

    def fox_q(p, j):
        hh, base = 2 * p + j, (1 - j) * HALF + F_FOX
        ones = jnp.where((lane >= base + 3) & (lane < base + 6), 1.0, 0.0)
        return jnp.where(lane == base, bcast(chi, hh),
               jnp.where(lane == base + 1, bcast(cmid, hh),
               jnp.where(lane == base + 2, bcast(clo, hh), ones)))

    def fox_k(p, j):
        hh, base = 2 * p + j, (1 - j) * HALF + F_FOX
        ones = jnp.where((lane >= base) & (lane < base + 3), 1.0, 0.0)
        return jnp.where(lane == base + 3, -bcast(chi, hh),
               jnp.where(lane == base + 4, -bcast(cmid, hh),
               jnp.where(lane == base + 5, -bcast(clo, hh), ones)))

    qb = _head_norm(proj(1536, 512), gqb_ref[...] * SCALE, e_ref)
    _store_blocks(qb_ref, qb, fox_q)
    kb = _head_norm(proj(2048, 512), gkb_ref[...], e_ref)
    _store_blocks(kb_ref, kb, fox_k)
    gb_ref[...] = proj(2560, 512).astype(BF16)


def _in_ab(x2, res, ng, w, wvt, gqa, gka, gqb, gkb, bf, e256, tri, *, seq):
    n = x2.shape[0]
    nt = n // TM
    row = lambda width: pl.BlockSpec((TM, width), lambda i: (i, 0))
    full = lambda a: pl.BlockSpec(a.shape, lambda i: (0,) * a.ndim)
    bf16 = lambda width: jax.ShapeDtypeStruct((n, width), BF16)
    res_in, res_in_specs, res_out_specs, res_out_shapes = _residual_specs(x2, res)
    consts = [ng, w, wvt, gqa, gka, gqb, gkb, bf, e256, tri]
    return pl.pallas_call(
        functools.partial(_in_ab_kernel, tiles_per_seq=seq // TM, fused=res is not None),
        grid=(nt,),
        in_specs=res_in_specs + [full(a) for a in consts],
        out_specs=res_out_specs + [row(1024), row(1024), row(512), row(1024), row(1024), row(512),
                                   pl.BlockSpec((wvt.shape[0], TM), lambda i: (0, i)),
                                   pl.BlockSpec((1, TM // MOBA_BLOCK, 512), lambda i: (i, 0, 0))],
        out_shape=res_out_shapes + [bf16(1024), bf16(1024), bf16(512), bf16(1024), bf16(1024), bf16(512),
                                    jax.ShapeDtypeStruct((wvt.shape[0], n), BF16),
                                    jax.ShapeDtypeStruct((nt, TM // MOBA_BLOCK, 512), F32)],
        scratch_shapes=[pltpu.VMEM((1, LANES), F32)],
        compiler_params=_cparams(("arbitrary",)),
        name="in_proj_ab",
    )(*res_in, *consts)


def _in_cd_kernel(*refs, tiles_per_seq, fused):
    x, refs = _residual_tile(refs, fused, 8)
    (ng_ref, w_ref, wvt_ref, gqc_ref, gkc_ref, gqd_ref, gkd_ref, e_ref,
     qc_ref, kc_ref, gc_ref, qd_ref, kd_ref, gd_ref, vt_ref) = refs
    i = pl.program_id(0)
    h = _rmsnorm_rows(x, ng_ref[...]).astype(BF16)

    def proj(c0, width):
        return jnp.dot(h, w_ref[:, c0:c0 + width], preferred_element_type=F32)

    _, pf = _pos_features(i, tiles_per_seq)
    dslopes = _alibi_slopes(4)
    sslopes = _alibi_slopes(8)
    vt_ref[...] = _nt(wvt_ref[...], h).astype(BF16)

    qc = _head_norm(proj(0, 512), gqc_ref[...] * SCALE, e_ref)
    _store_blocks(qc_ref, qc, lambda p, j: pf[j][0] * dslopes[p])
    kc = _head_norm(proj(512, 512), gkc_ref[...], e_ref)
    _store_blocks(kc_ref, kc, lambda p, j: pf[j][1])
    gc_ref[...] = proj(1024, 512).astype(BF16)

    qd = _head_norm(proj(1536, 512), gqd_ref[...] * SCALE, e_ref)
    _store_blocks(qd_ref, qd, lambda p, j: pf[j][0] * sslopes[p + 4 * j])
    kd = _head_norm(proj(2048, LANES), gkd_ref[...], e_ref)
    _store_blocks(kd_ref, kd, lambda p, j: pf[j][1])
    gd_ref[...] = proj(2176, 512).astype(BF16)


def _in_cd(x2, res, ng, w, wvt, gqc, gkc, gqd, gkd, e256, *, seq):
    n = x2.shape[0]
    nt = n // TM
    row = lambda width: pl.BlockSpec((TM, width), lambda i: (i, 0))
    full = lambda a: pl.BlockSpec(a.shape, lambda i: (0,) * a.ndim)
    bf16 = lambda width: jax.ShapeDtypeStruct((n, width), BF16)
    res_in, res_in_specs, res_out_specs, res_out_shapes = _residual_specs(x2, res)
    consts = [ng, w, wvt, gqc, gkc, gqd, gkd, e256]
    return pl.pallas_call(
        functools.partial(_in_cd_kernel, tiles_per_seq=seq // TM, fused=res is not None),
        grid=(nt,),
        in_specs=res_in_specs + [full(a) for a in consts],
        out_specs=res_out_specs + [row(1024), row(1024), row(512), row(1024), row(256), row(512),
                                   pl.BlockSpec((wvt.shape[0], TM), lambda i: (0, i))],
        out_shape=res_out_shapes + [bf16(1024), bf16(1024), bf16(512), bf16(1024), bf16(256), bf16(512),
                                    jax.ShapeDtypeStruct((wvt.shape[0], n), BF16)],
        compiler_params=_cparams(("arbitrary",)),
        name="in_proj_cd",
    )(*res_in, *consts)


def _out_kernel(x_ref, ya_ref, yb_ref, w_ref, o_ref):
    half = ya_ref.shape[1]
    o_ref[...] = (x_ref[...]
                  + jnp.dot(ya_ref[...], w_ref[:half, :], preferred_element_type=F32)
                  + jnp.dot(yb_ref[...], w_ref[half:, :], preferred_element_type=F32))


def _out_proj(x2, ya, yb, w):
    n, d = x2.shape
    row = lambda width: pl.BlockSpec((TM, width), lambda i: (i, 0))
    return pl.pallas_call(
        _out_kernel,
        grid=(n // TM,),
        in_specs=[row(d), row(ya.shape[1]), row(yb.shape[1]), pl.BlockSpec(w.shape, lambda i: (0, 0))],
        out_specs=row(d),
        out_shape=jax.ShapeDtypeStruct((n, d), F32),
        compiler_params=_cparams(("arbitrary",)),
        name="out_proj",
    )(x2, ya, yb, w)


def _silu(g):
    return g * (1.0 / (1.0 + jnp.exp(-g)))


def _attend(pt, vt):
    lhs = jnp.concatenate([vt, jnp.ones((BF16_ROWS, vt.shape[1]), BF16)], axis=0)
    return jnp.dot(lhs, pt, preferred_element_type=F32)


def _causal_attend(st_main, st_last, vt, causal, sub_max):
    assert TQ == 2 * TK
    nm = st_main.shape[0]
    tri = causal[:TK, :TK]
    if sub_max:
        dark = jnp.full((TK, TQ - TK), MASK_NEG, F32)
        st = jnp.concatenate([st_main, jnp.concatenate([dark, st_last], axis=1)], axis=0)
        tail = jnp.where(causal, st[nm - TK:], MASK_NEG)
        st = tail if nm == TK else jnp.concatenate([st[:nm - TK], tail], axis=0)
        return _attend(jnp.exp(st - jnp.max(st, axis=0, keepdims=True)).astype(BF16), vt)
    own = st_main[nm - TK:]
    own = jnp.concatenate([jnp.where(tri, own[:, :TK], MASK_NEG), own[:, TK:]], axis=1)
    parts = ([st_main[:nm - TK]] if nm > TK else []) + [own]
    ot = _attend(jnp.concatenate([jnp.exp(p).astype(BF16) for p in parts], axis=0), vt[:, :nm])
    ot_last = _attend(jnp.exp(jnp.where(tri, st_last, MASK_NEG)).astype(BF16), vt[:, nm:])
    return jnp.concatenate([ot[:, :TK], ot[:, TK:] + ot_last], axis=1)


def _normalized(ot, dims):
    return ot[:dims] / ot[dims:dims + 1]


def _causal_mask_t():
    key = lax.broadcasted_iota(jnp.int32, (TQ, TQ), 0)
    qry = lax.broadcasted_iota(jnp.int32, (TQ, TQ), 1)
    return key <= qry


def _own_block(qi):
    return qi * (TQ // TK) + lax.broadcasted_iota(jnp.int32, (1, TQ), 1) // TK


def _moba_bias(gate, qi):
    first = qi * (TQ // TK)
    own = _own_block(qi)
    n_iota = lax.broadcasted_iota(jnp.int32, (8, TQ), 0)
    cnt = jnp.zeros((8, TQ), F32)
    for n2 in range(first + TQ // TK - 1):
        g2 = jnp.broadcast_to(gate[n2:n2 + 1, :], (8, TQ))
        beats = (g2 > gate) | ((g2 == gate) & (n_iota > n2))
        if n2 >= first:
            beats = beats & (own > n2)
        cnt = cnt + jnp.where(beats, 1.0, 0.0)
    return jnp.where(cnt < MOBA_TOPK, 0.0, SEL_NEG)


def _attn_ab_kernel(*refs, moba, nq, sub_max):
    if moba:
        q_ref, k_ref, vt_ref, g_ref, km_ref, o_ref = refs
    else:
        q_ref, k_ref, vt_ref, g_ref, o_ref = refs
    causal = _causal_mask_t()
    km16 = {}
    if moba:
        lane8 = lax.broadcasted_iota(jnp.int32, (8, LANES), 1)
        for u in range(PAIRS):
            for j in (0, 1):
                km = km_ref[0, :, u * LANES:(u + 1) * LANES]
                km = jnp.where((lane8 >= HALF) if j == 1 else (lane8 < HALF), km, 0.0)
                hi = km.astype(BF16)
                lo = (km - hi.astype(F32)).astype(BF16)
                km16[u, j] = jnp.concatenate([hi, lo], axis=0)
    sub = TQ // TK
    for qi in range(nq):
        r0, n = qi * TQ, (qi + 1) * TQ
        first = qi * sub
        for u in range(PAIRS):
            outs = []
            for j in (0, 1):
                c0 = (2 * u + j) * LANES
                qj = q_ref[0, r0:r0 + TQ, c0:c0 + LANES]
                kk = k_ref[0, :n - TK, c0:c0 + LANES]
                if moba and first + sub - 1 > MOBA_TOPK:
                    st = _nt(jnp.concatenate([km16[u, j], kk], axis=0), qj)
                    bias = _moba_bias(st[:8] + st[8:16], qi)
                    st = st[16:]
                    blocks = [st[nb * TK:(nb + 1) * TK] + bias[nb:nb + 1, :] for nb in range(first)]
                    blocks.append(st[first * TK:] + jnp.where(_own_block(qi) > first, bias[first:first + 1, :], 0.0))
                    st = jnp.concatenate(blocks, axis=0)
                else:
                    st = _nt(kk, qj)
                st_last = _nt(k_ref[0, n - TK:n, c0:c0 + LANES], qj[TQ - TK:])
                v0 = (2 * u + j) * HALF
                ot = _causal_attend(st, st_last, vt_ref[v0:v0 + HALF, :n], causal, sub_max)
                outs.append(_normalized(ot, HALF))
            o = jnp.concatenate(outs, axis=0).T
            gate = _silu(g_ref[0, r0:r0 + TQ, u * LANES:(u + 1) * LANES].astype(F32))
            o_ref[0, r0:r0 + TQ, u * LANES:(u + 1) * LANES] = (o * gate).astype(BF16)


def _step_spec(s, width):
    return pl.BlockSpec((1, s, PAIRS * width), lambda bi, p: (bi, 0, p))


def _vt_spec(s, row0):
    rows = PAIRS * LANES
    return pl.BlockSpec((rows, s), lambda bi, p: (row0 // rows + p, bi))


def _attn_ab(q, k, vt, g, km, *, moba, sub_max):
    b, s, _ = q.shape
    in_specs = [_step_spec(s, 2 * LANES), _step_spec(s, 2 * LANES), _vt_spec(s, 0 if moba else 4 * LANES),
                _step_spec(s, LANES)]
    args = [q, k, vt, g]
    if moba:
        in_specs.append(pl.BlockSpec((1, 8, PAIRS * LANES), lambda bi, p: (bi, 0, p)))
        args.append(km)
    return pl.pallas_call(
        functools.partial(_attn_ab_kernel, moba=moba, nq=s // TQ, sub_max=sub_max),
        grid=(b, 4 // PAIRS),
        in_specs=in_specs,
        out_specs=_step_spec(s, LANES),
        out_shape=jax.ShapeDtypeStruct((b, s, 4 * LANES), BF16),
        compiler_params=_cparams(("arbitrary", "arbitrary")),
        name=("attn_moba" if moba else "attn_fox") + ("_submax" if sub_max else ""),
    )(*args)


def _attn_diff_kernel(q_ref, k_ref, vt_ref, g_ref, lam_ref, sg_ref, o_ref, *, lam_init, nq, sub_max):
    lp = lam_ref[...]
    s1 = jnp.sum(lp[0:1] * lp[1:2], axis=-1, keepdims=True)
    s2 = jnp.sum(lp[2:3] * lp[3:4], axis=-1, keepdims=True)
    lam = jnp.exp(s1) - jnp.exp(s2) + lam_init
    causal = _causal_mask_t()
    for qi in range(nq):
        r0, n = qi * TQ, (qi + 1) * TQ
        for u in range(PAIRS):
            outs = []
            for j in (0, 1):
                c0 = (2 * u + j) * LANES
                qj = q_ref[0, r0:r0 + TQ, c0:c0 + LANES]
                st = _nt(k_ref[0, :n - TK, c0:c0 + LANES], qj)
                st_last = _nt(k_ref[0, n - TK:n, c0:c0 + LANES], qj[TQ - TK:])
                ot = _causal_attend(st, st_last, vt_ref[u * LANES:(u + 1) * LANES, :n], causal, sub_max)
                outs.append(_normalized(ot, LANES))
            oc = (outs[0] - lam * outs[1]).T
            oc = _rmsnorm_rows(oc, sg_ref[...]) * (1.0 - lam_init)
            gate = _silu(g_ref[0, r0:r0 + TQ, u * LANES:(u + 1) * LANES].astype(F32))
            o_ref[0, r0:r0 + TQ, u * LANES:(u + 1) * LANES] = (oc * gate).astype(BF16)


def _attn_diff(q, k, vt, g, lam, sg, *, lam_init, sub_max):
    b, s, _ = q.shape
    return pl.pallas_call(
        functools.partial(_attn_diff_kernel, lam_init=lam_init, nq=s // TQ, sub_max=sub_max),
        grid=(b, 4 // PAIRS),
        in_specs=[_step_spec(s, 2 * LANES), _step_spec(s, 2 * LANES), _vt_spec(s, 0),
                  _step_spec(s, LANES),
                  pl.BlockSpec(lam.shape, lambda bi, p: (0, 0)),
                  pl.BlockSpec(sg.shape, lambda bi, p: (0, 0))],
        out_specs=_step_spec(s, LANES),
        out_shape=jax.ShapeDtypeStruct((b, s, 4 * LANES), BF16),
        compiler_params=_cparams(("arbitrary", "arbitrary")),
        name="attn_diff" + ("_submax" if sub_max else ""),
    )(q, k, vt, g, lam, sg)


def _attn_swa_kernel(q_ref, k_ref, vt_ref, g_ref, sink_ref, o_ref, *, nt, sub_max):
    w = SWA_BLOCK
    key = lax.broadcasted_iota(jnp.int32, (w, w), 0)
    qry = lax.broadcasted_iota(jnp.int32, (w, w), 1)
    bias = {0: jnp.where(key <= qry, 0.0, MASK_NEG),
            1: jnp.where(key > qry, 0.0, MASK_NEG)}
    dark = jnp.full((w, w), MASK_NEG, F32)
    for ti in range(nt):
        t0 = ti * TS
        start = max(t0 - w, 0)
        nkeys = t0 + TS - start
        heads = []
        for j in (0, 1):
            qs = jnp.concatenate([q_ref[0, t0:t0 + TS, (2 * p + j) * LANES:(2 * p + j + 1) * LANES]
                                  for p in range(4)], axis=0)
            st = _nt(k_ref[0, start:start + nkeys, j * LANES:(j + 1) * LANES], qs)
            sink = sink_ref[j:j + 1, :]

            def blocks(fn):
                rows = []
                for r in range(nkeys // w):
                    cols = []
                    for c in range(4 * TS // w):
                        dist = (t0 // w + c % (TS // w)) - (start // w + r)
                        cols.append(fn(st[r * w:(r + 1) * w, c * w:(c + 1) * w], dist))
                    rows.append(jnp.concatenate(cols, axis=1))
                return jnp.concatenate(rows, axis=0)

            if sub_max:
                st = blocks(lambda blk, d: blk + bias.get(d, dark))
                m = jnp.maximum(jnp.max(st, axis=0, keepdims=True), sink)
                pt, sink = jnp.exp(st - m).astype(BF16), sink - m
            else:
                pt = blocks(lambda blk, d: jnp.exp(blk + bias[d]).astype(BF16) if d in bias
                            else jnp.zeros((w, w), BF16))
            ot = _attend(pt, vt_ref[j * HALF:(j + 1) * HALF, start:start + nkeys])
            heads.append(ot[:HALF] / (ot[HALF:HALF + 1] + jnp.exp(sink)))
        for p in range(4):
            o = jnp.concatenate([hd[:, p * TS:(p + 1) * TS] for hd in heads], axis=0).T
            gate = _silu(g_ref[0, t0:t0 + TS, p * LANES:(p + 1) * LANES].astype(F32))
            o_ref[0, t0:t0 + TS, p * LANES:(p + 1) * LANES] = (o * gate).astype(BF16)


def _attn_swa(q, k, vt, g, sinks, *, sub_max):
    b, s, _ = q.shape
    whole = lambda width: pl.BlockSpec((1, s, width), lambda bi: (bi, 0, 0))
    return pl.pallas_call(
        functools.partial(_attn_swa_kernel, nt=s // TS, sub_max=sub_max),
        grid=(b,),
        in_specs=[whole(8 * LANES), whole(2 * LANES), pl.BlockSpec((LANES, s), lambda bi: (4, bi)),
                  whole(4 * LANES), pl.BlockSpec(sinks.shape, lambda bi: (0, 0))],
        out_specs=whole(4 * LANES),
        out_shape=jax.ShapeDtypeStruct((b, s, 4 * LANES), BF16),
        compiler_params=_cparams(("arbitrary",)),
        name="attn_swa" + ("_submax" if sub_max else ""),
    )(q, k, vt, g, sinks)


SWA_HEAD_ORDER = (0, 4, 1, 5, 2, 6, 3, 7)


def _tile_gain(g, reps):
    return jnp.tile(g.astype(F32), reps)[None, :]


def _logit_bound(gq, gk):
    return HEAD_DIM * SCALE * jnp.max(jnp.abs(gq)) * jnp.max(jnp.abs(gk))


def _guarded(bound, fn, *args):
    return lax.cond(bound <= LOGIT_SAFE, functools.partial(fn, sub_max=False),
                    functools.partial(fn, sub_max=True), *args)


def kernel(x, norm_gain, w_in_ab, b_forget, moba_q_gain, moba_k_gain, fox_q_gain, fox_k_gain, w_out_ab,
           w_in_cd, diff_q_gain, diff_k_gain, diff_lambda, diff_subln_gain, swa_q_gain, swa_k_gain,
           swa_sinks, w_out_cd):
    b, s, d = x.shape
    assert d == 1024 and s % TM == 0 and s // MOBA_BLOCK == 8
    depth = norm_gain.shape[0]
    x2 = x.reshape(b * s, d)

    grp = np.arange(256) // HEAD_DIM
    e256 = jnp.asarray(grp[:, None] == grp[None, :], dtype=BF16)
    tri = jnp.asarray(np.tril(np.ones((256, 256), np.float32)), dtype=BF16)
    perm = np.concatenate([np.arange(HEAD_DIM) + HEAD_DIM * hh for hh in SWA_HEAD_ORDER])
    r3 = lambda a: a.reshape(b, s, a.shape[1])

    res = None
    for layer in range(depth):
        j = layer // 2
        ng = norm_gain[layer][None, :]
        if layer % 2 == 0:
            wj = w_in_ab[j]
            w = jnp.concatenate([wj[:, 0:1024], wj[:, 1536:2048], wj[:, 2048:3072], wj[:, 3584:4096],
                                 jnp.pad(wj[:, 4096:], ((0, 0), (0, LANES - (wj.shape[1] - 4096))))],
                                axis=1).astype(BF16)
            wvt = jnp.concatenate([wj[:, 1024:1536], wj[:, 3072:3584]], axis=1).T.astype(BF16)
            bf = jnp.pad(b_forget[j].astype(F32), (0, LANES - b_forget.shape[1]))[None, :]
            outs = _in_ab(
                x2, res, ng, w, wvt, _tile_gain(moba_q_gain[j], 8), _tile_gain(moba_k_gain[j], 8),
                _tile_gain(fox_q_gain[j], 8), _tile_gain(fox_k_gain[j], 8), bf, e256, tri, seq=s)
            if res is not None:
                x2, outs = outs[0], outs[1:]
            qa, ka, ga, qb, kb, gb, vt, km = outs
            ya = _guarded(_logit_bound(moba_q_gain[j], moba_k_gain[j]), functools.partial(_attn_ab, moba=True),
                          r3(qa), r3(ka), vt, r3(ga), km.reshape(b, 8, 512))
            yb = _guarded(_logit_bound(fox_q_gain[j], fox_k_gain[j]), functools.partial(_attn_ab, moba=False),
                          r3(qb), r3(kb), vt, r3(gb), None)
            res = (ya.reshape(b * s, 512), yb.reshape(b * s, 512), w_out_ab[j].astype(BF16))
        else:
            wj = w_in_cd[j]
            w = jnp.concatenate([wj[:, 0:1024], wj[:, 1536:2048], wj[:, 2048:2560][:, perm],
                                 wj[:, 2560:2688], wj[:, 2816:3328][:, perm]], axis=1).astype(BF16)
            wvt = jnp.concatenate([wj[:, 1024:1536], wj[:, 2688:2816]], axis=1).T.astype(BF16)
            outs = _in_cd(
                x2, res, ng, w, wvt, _tile_gain(diff_q_gain[j], 8), _tile_gain(diff_k_gain[j], 8),
                _tile_gain(swa_q_gain[j], 8), _tile_gain(swa_k_gain[j], 2), e256, seq=s)
            if res is not None:
                x2, outs = outs[0], outs[1:]
            qc, kc, gc, qd, kd, gd, vt = outs
            lam_init = 0.8 - 0.6 * math.exp(-0.3 * layer)
            yc = _guarded(_logit_bound(diff_q_gain[j], diff_k_gain[j]),
                          functools.partial(_attn_diff, lam_init=lam_init),
                          r3(qc), r3(kc), vt, r3(gc), diff_lambda[j].astype(F32),
                          diff_subln_gain[j].astype(F32)[None, :])
            sinks = swa_sinks[j].astype(F32).reshape(2, 4)
            swa_bound = jnp.maximum(_logit_bound(swa_q_gain[j], swa_k_gain[j]), jnp.max(sinks))
            yd = _guarded(swa_bound, _attn_swa,
                          r3(qd), r3(kd), vt, r3(gd), jnp.repeat(sinks, TS, axis=1))
            wo = w_out_cd[j]
            wo = jnp.concatenate([wo[:512], wo[512:][perm]], axis=0).astype(BF16)
            res = (yc.reshape(b * s, 512), yd.reshape(b * s, 512), wo)
    return _out_proj(x2, *res).reshape(b, s, d)
```

```python
import functools
import math

import numpy as np
import jax
import jax.numpy as jnp
from jax import lax
from jax.experimental import pallas as pl
from jax.experimental.pallas import tpu as pltpu

F32 = jnp.float32
BF16 = jnp.bfloat16

LANES = 128
BF16_ROWS = 16
HEAD_DIM = 64
HALF = 64
EPS = 1e-6
SCALE = HEAD_DIM ** -0.5
TM = 512
TM_OUT = 1024
TQ = 512
TK = 256
PAIRS = 2
TS = 256
SWA_BLOCK = 128
MOBA_BLOCK = 256
MOBA_TOPK = 3
MASK_NEG = -1e30
SEL_NEG = -1e9
LOGIT_SAFE = 32.0
VMEM_LIMIT = 56 * 1024 * 1024

F_ALIBI = 0
F_FOX = 0


def _alibi_slopes(n):
    s = [2.0 ** (-8.0 * (i + 1) / n) for i in range(n)]
    for v in s:
        m, _ = math.frexp(v)
        assert m == 0.5, "feature-lane ALiBi needs power-of-two slopes"
    return s


def _cparams(sem):
    return pltpu.CompilerParams(dimension_semantics=sem, vmem_limit_bytes=VMEM_LIMIT)


def _nt(a, b):
    return lax.dot_general(a, b, (((1,), (1,)), ((), ())), preferred_element_type=F32)


def _group_sums(sq, e_ref):
    width = sq.shape[1]
    if width == LANES:
        return jnp.dot(sq.astype(BF16), e_ref[:LANES, :LANES], preferred_element_type=F32)
    e = e_ref[...]
    outs = [jnp.dot(sq[:, c:c + 256].astype(BF16), e, preferred_element_type=F32)
            for c in range(0, width, 256)]
    return outs[0] if len(outs) == 1 else jnp.concatenate(outs, axis=1)


def _head_norm(p, gain_row, e_ref):
    ssq = _group_sums(p * p, e_ref)
    return p * lax.rsqrt(ssq * (1.0 / HEAD_DIM) + EPS) * gain_row


def _lane_iota(rows):
    return lax.broadcasted_iota(jnp.int32, (rows, LANES), 1)


def _pos_features(i, tiles_per_seq):
    lane = _lane_iota(TM)
    pos = (i % tiles_per_seq) * TM + lax.broadcasted_iota(jnp.int32, (TM, LANES), 0)
    a64 = ((pos >> 6) << 6).astype(F32)
    a = (pos >> 6).astype(F32)
    b = (pos & 63).astype(F32)
    out = []
    for j in (0, 1):
        base = (1 - j) * HALF + F_ALIBI
        qb = jnp.where(lane == base, -a64,
             jnp.where(lane == base + 1, -b,
             jnp.where(lane == base + 2, 64.0,
             jnp.where(lane == base + 3, 1.0, 0.0))))
        kf = jnp.where((lane == base) | (lane == base + 1), 1.0,
             jnp.where(lane == base + 2, a,
             jnp.where(lane == base + 3, b, 0.0)))
        out.append((qb, kf))
    return lane, out


def _rmsnorm_rows(x, gain_row):
    return x * lax.rsqrt(jnp.mean(x * x, axis=-1, keepdims=True) + EPS) * gain_row


def _residual_tile(refs, fused, n_in):
    if not fused:
        return refs[0][...], refs[1:]
    x_ref, ya_ref, yb_ref, wo_ref = refs[:4]
    rest = refs[4:]
    half = ya_ref.shape[1]
    x = (x_ref[...]
         + jnp.dot(ya_ref[...], wo_ref[:half, :], preferred_element_type=F32)
         + jnp.dot(yb_ref[...], wo_ref[half:, :], preferred_element_type=F32))
    rest[n_in][...] = x
    return x, rest[:n_in] + rest[n_in + 1:]


def _residual_specs(x2, res):
    row = lambda width: pl.BlockSpec((TM, width), lambda i: (i, 0))
    if res is None:
        return [x2], [row(x2.shape[1])], [], []
    ya, yb, wo = res
    return ([x2, ya, yb, wo],
            [row(x2.shape[1]), row(ya.shape[1]), row(yb.shape[1]), pl.BlockSpec(wo.shape, lambda i: (0, 0))],
            [row(x2.shape[1])], [jax.ShapeDtypeStruct(x2.shape, F32)])


def _store_blocks(dst_ref, src, feats):
    lane = _lane_iota(src.shape[0])
    for p in range(src.shape[1] // LANES):
        pair = src[:, p * LANES:(p + 1) * LANES]
        for j in (0, 1):
            data = (lane >= HALF) if j == 1 else (lane < HALF)
            blk = jnp.where(data, pair, feats(p, j))
            dst_ref[:, (2 * p + j) * LANES:(2 * p + j + 1) * LANES] = blk.astype(BF16)


def _in_ab_kernel(*refs, tiles_per_seq, fused):
    x, refs = _residual_tile(refs, fused, 10)
    (ng_ref, w_ref, wvt_ref, gqa_ref, gka_ref, gqb_ref, gkb_ref, bf_ref, e_ref, tri_ref,
     qa_ref, ka_ref, ga_ref, qb_ref, kb_ref, gb_ref, vt_ref, km_ref, carry_ref) = refs
    i = pl.program_id(0)

    @pl.when(i % tiles_per_seq == 0)
    def _():
        carry_ref[...] = jnp.zeros_like(carry_ref)

    h = _rmsnorm_rows(x, ng_ref[...]).astype(BF16)

    def proj(c0, width):
        return jnp.dot(h, w_ref[:, c0:c0 + width], preferred_element_type=F32)

    lane, pf = _pos_features(i, tiles_per_seq)
    slopes = _alibi_slopes(8)
    rows_t = _nt(wvt_ref[...], h)
    nv = vt_ref.shape[0]
    vt_ref[...] = rows_t[:nv].astype(BF16)

    qa = _head_norm(proj(0, 512), gqa_ref[...] * SCALE, e_ref)
    _store_blocks(qa_ref, qa, lambda p, j: pf[j][0] * slopes[2 * p + j])
    ka = _head_norm(proj(512, 512), gka_ref[...], e_ref)
    _store_blocks(ka_ref, ka, lambda p, j: pf[j][1])
    km = [jnp.mean(ka[r * MOBA_BLOCK:(r + 1) * MOBA_BLOCK], axis=0, keepdims=True)
          for r in range(TM // MOBA_BLOCK)]
    km_ref[0] = jnp.concatenate(km, axis=0)
    ga_ref[...] = proj(1024, 512).astype(BF16)

    z = rows_t[nv:] + bf_ref[...]
    logf = jnp.minimum(z, 0.0) - jnp.log1p(jnp.exp(-jnp.abs(z)))
    hi = logf.astype(BF16)
    r1 = logf - hi.astype(F32)
    mid = r1.astype(BF16)
    lo = (r1 - mid.astype(F32)).astype(BF16)
    cs = jnp.dot(jnp.concatenate([hi, mid, lo], axis=0), tri_ref[...], preferred_element_type=F32)
    nh = logf.shape[0]
    ct = cs[:nh] + cs[nh:2 * nh] + cs[2 * nh:] + carry_ref[:, 0:1]
    carry_ref[...] = jnp.broadcast_to(ct[:, TM - 1:TM], carry_ref.shape)
    c = jnp.concatenate([ct, jnp.zeros((LANES - nh, TM), F32)], axis=0).T
    chi = c.astype(BF16).astype(F32)
    r1 = c - chi
    cmid = r1.astype(BF16).astype(F32)
    clo = r1 - cmid

    def bcast(arr, hh):
        return jnp.broadcast_to(arr[:, hh:hh + 1], (TM, LANES))

    def fox_q(p, j):
        hh, base = 2 * p + j, (1 - j) * HALF + F_FOX
        ones = jnp.where((lane >= base + 3) & (lane < base + 6), 1.0, 0.0)
        return jnp.where(lane == base, bcast(chi, hh),
               jnp.where(lane == base + 1, bcast(cmid, hh),
               jnp.where(lane == base + 2, bcast(clo, hh), ones)))

    def fox_k(p, j):
        hh, base = 2 * p + j, (1 - j) * HALF + F_FOX
        ones = jnp.where((lane >= base) & (lane < base + 3), 1.0, 0.0)
        return jnp.where(lane == base + 3, -bcast(chi, hh),
               jnp.where(lane == base + 4, -bcast(cmid, hh),
               jnp.where(lane == base + 5, -bcast(clo, hh), ones)))

    qb = _head_norm(proj(1536, 512), gqb_ref[...] * SCALE, e_ref)
    _store_blocks(qb_ref, qb, fox_q)
    kb = _head_norm(proj(2048, 512), gkb_ref[...], e_ref)
    _store_blocks(kb_ref, kb, fox_k)
    gb_ref[...] = proj(2560, 512).astype(BF16)


def _in_ab(x2, res, ng, w, wvt, gqa, gka, gqb, gkb, bf, e256, tri, *, seq):
    n = x2.shape[0]
    nt = n // TM
    row = lambda width: pl.BlockSpec((TM, width), lambda i: (i, 0))
    full = lambda a: pl.BlockSpec(a.shape, lambda i: (0,) * a.ndim)
    bf16 = lambda width: jax.ShapeDtypeStruct((n, width), BF16)
    res_in, res_in_specs, res_out_specs, res_out_shapes = _residual_specs(x2, res)
    consts = [ng, w, wvt, gqa, gka, gqb, gkb, bf, e256, tri]
    return pl.pallas_call(
        functools.partial(_in_ab_kernel, tiles_per_seq=seq // TM, fused=res is not None),
        grid=(nt,),
        in_specs=res_in_specs + [full(a) for a in consts],
        out_specs=res_out_specs + [row(1024), row(1024), row(512), row(1024), row(1024), row(512),
                                   pl.BlockSpec((wvt.shape[0] - BF16_ROWS, TM), lambda i: (0, i)),
                                   pl.BlockSpec((1, TM // MOBA_BLOCK, 512), lambda i: (i, 0, 0))],
        out_shape=res_out_shapes + [bf16(1024), bf16(1024), bf16(512), bf16(1024), bf16(1024), bf16(512),
                                    jax.ShapeDtypeStruct((wvt.shape[0] - BF16_ROWS, n), BF16),
                                    jax.ShapeDtypeStruct((nt, TM // MOBA_BLOCK, 512), F32)],
        scratch_shapes=[pltpu.VMEM((BF16_ROWS, LANES), F32)],
        compiler_params=_cparams(("arbitrary",)),
        name="in_proj_ab",
    )(*res_in, *consts)


def _in_cd_kernel(*refs, tiles_per_seq, fused):
    x, refs = _residual_tile(refs, fused, 8)
    (ng_ref, w_ref, wvt_ref, gqc_ref, gkc_ref, gqd_ref, gkd_ref, e_ref,
     qc_ref, kc_ref, gc_ref, qd_ref, kd_ref, gd_ref, vt_ref) = refs
    i = pl.program_id(0)
    h = _rmsnorm_rows(x, ng_ref[...]).astype(BF16)

    def proj(c0, width):
        return jnp.dot(h, w_ref[:, c0:c0 + width], preferred_element_type=F32)

    _, pf = _pos_features(i, tiles_per_seq)
    dslopes = _alibi_slopes(4)
    sslopes = _alibi_slopes(8)
    vt_ref[...] = _nt(wvt_ref[...], h).astype(BF16)

    qc = _head_norm(proj(0, 512), gqc_ref[...] * SCALE, e_ref)
    _store_blocks(qc_ref, qc, lambda p, j: pf[j][0] * dslopes[p])
    kc = _head_norm(proj(512, 512), gkc_ref[...], e_ref)
    _store_blocks(kc_ref, kc, lambda p, j: pf[j][1])
    gc_ref[...] = proj(1024, 512).astype(BF16)

    qd = _head_norm(proj(1536, 512), gqd_ref[...] * SCALE, e_ref)
    _store_blocks(qd_ref, qd, lambda p, j: pf[j][0] * sslopes[p + 4 * j])
    kd = _head_norm(proj(2048, LANES), gkd_ref[...], e_ref)
    _store_blocks(kd_ref, kd, lambda p, j: pf[j][1])
    gd_ref[...] = proj(2176, 512).astype(BF16)


def _in_cd(x2, res, ng, w, wvt, gqc, gkc, gqd, gkd, e256, *, seq):
    n = x2.shape[0]
    nt = n // TM
    row = lambda width: pl.BlockSpec((TM, width), lambda i: (i, 0))
    full = lambda a: pl.BlockSpec(a.shape, lambda i: (0,) * a.ndim)
    bf16 = lambda width: jax.ShapeDtypeStruct((n, width), BF16)
    res_in, res_in_specs, res_out_specs, res_out_shapes = _residual_specs(x2, res)
    consts = [ng, w, wvt, gqc, gkc, gqd, gkd, e256]
    return pl.pallas_call(
        functools.partial(_in_cd_kernel, tiles_per_seq=seq // TM, fused=res is not None),
        grid=(nt,),
        in_specs=res_in_specs + [full(a) for a in consts],
        out_specs=res_out_specs + [row(1024), row(1024), row(512), row(1024), row(256), row(512),
                                   pl.BlockSpec((wvt.shape[0], TM), lambda i: (0, i))],
        out_shape=res_out_shapes + [bf16(1024), bf16(1024), bf16(512), bf16(1024), bf16(256), bf16(512),
                                    jax.ShapeDtypeStruct((wvt.shape[0], n), BF16)],
        compiler_params=_cparams(("arbitrary",)),
        name="in_proj_cd",
    )(*res_in, *consts)


def _out_kernel(x_ref, ya_ref, yb_ref, w_ref, o_ref):
    half = ya_ref.shape[1]
    o_ref[...] = (x_ref[...]
                  + jnp.dot(ya_ref[...], w_ref[:half, :], preferred_element_type=F32)
                  + jnp.dot(yb_ref[...], w_ref[half:, :], preferred_element_type=F32))


def _out_proj(x2, ya, yb, w):
    n, d = x2.shape
    row = lambda width: pl.BlockSpec((TM_OUT, width), lambda i: (i, 0))
    return pl.pallas_call(
        _out_kernel,
        grid=(n // TM_OUT,),
        in_specs=[row(d), row(ya.shape[1]), row(yb.shape[1]), pl.BlockSpec(w.shape, lambda i: (0, 0))],
        out_specs=row(d),
        out_shape=jax.ShapeDtypeStruct((n, d), F32),
        compiler_params=_cparams(("arbitrary",)),
        name="out_proj",
    )(x2, ya, yb, w)


def _silu(g):
    return g * (1.0 / (1.0 + jnp.exp(-g)))


def _attend(pt, vt):
    lhs = jnp.concatenate([vt, jnp.ones((BF16_ROWS, vt.shape[1]), BF16)], axis=0)
    return jnp.dot(lhs, pt, preferred_element_type=F32)


def _causal_attend(st_main, st_last, vt, causal, sub_max):
    assert TQ == 2 * TK
    nm = st_main.shape[0]
    tri = causal[:TK, :TK]
    if sub_max:
        dark = jnp.full((TK, TQ - TK), MASK_NEG, F32)
        st = jnp.concatenate([st_main, jnp.concatenate([dark, st_last], axis=1)], axis=0)
        tail = jnp.where(causal, st[nm - TK:], MASK_NEG)
        st = tail if nm == TK else jnp.concatenate([st[:nm - TK], tail], axis=0)
        return _attend(jnp.exp(st - jnp.max(st, axis=0, keepdims=True)).astype(BF16), vt)
    own = st_main[nm - TK:]
    own = jnp.concatenate([jnp.where(tri, own[:, :TK], MASK_NEG), own[:, TK:]], axis=1)
    parts = ([st_main[:nm - TK]] if nm > TK else []) + [own]
    ot = _attend(jnp.concatenate([jnp.exp(p).astype(BF16) for p in parts], axis=0), vt[:, :nm])
    ot_last = _attend(jnp.exp(jnp.where(tri, st_last, MASK_NEG)).astype(BF16), vt[:, nm:])
    return jnp.concatenate([ot[:, :TK], ot[:, TK:] + ot_last], axis=1)


def _normalized(ot, dims):
    return ot[:dims] / ot[dims:dims + 1]


def _causal_mask_t():
    key = lax.broadcasted_iota(jnp.int32, (TQ, TQ), 0)
    qry = lax.broadcasted_iota(jnp.int32, (TQ, TQ), 1)
    return key <= qry


def _own_block(qi):
    return qi * (TQ // TK) + lax.broadcasted_iota(jnp.int32, (1, TQ), 1) // TK


def _moba_bias(gate, qi):
    first = qi * (TQ // TK)
    own = _own_block(qi)
    n_iota = lax.broadcasted_iota(jnp.int32, (8, TQ), 0)
    cnt = jnp.zeros((8, TQ), F32)
    for n2 in range(first + TQ // TK - 1):
        g2 = jnp.broadcast_to(gate[n2:n2 + 1, :], (8, TQ))
        beats = (g2 > gate) | ((g2 == gate) & (n_iota > n2))
        if n2 >= first:
            beats = beats & (own > n2)
        cnt = cnt + jnp.where(beats, 1.0, 0.0)
    return jnp.where(cnt < MOBA_TOPK, 0.0, SEL_NEG)


def _attn_ab_kernel(*refs, moba, nq, sub_max):
    if moba:
        q_ref, k_ref, vt_ref, g_ref, km_ref, o_ref = refs
    else:
        q_ref, k_ref, vt_ref, g_ref, o_ref = refs
    causal = _causal_mask_t()
    km16 = {}
    if moba:
        lane8 = lax.broadcasted_iota(jnp.int32, (8, LANES), 1)
        for u in range(PAIRS):
            for j in (0, 1):
                km = km_ref[0, :, u * LANES:(u + 1) * LANES]
                km = jnp.where((lane8 >= HALF) if j == 1 else (lane8 < HALF), km, 0.0)
                hi = km.astype(BF16)
                lo = (km - hi.astype(F32)).astype(BF16)
                km16[u, j] = jnp.concatenate([hi, lo], axis=0)
    sub = TQ // TK
    for qi in range(nq):
        r0, n = qi * TQ, (qi + 1) * TQ
        first = qi * sub
        for u in range(PAIRS):
            outs = []
            for j in (0, 1):
                c0 = (2 * u + j) * LANES
                qj = q_ref[0, r0:r0 + TQ, c0:c0 + LANES]
                kk = k_ref[0, :n - TK, c0:c0 + LANES]
                if moba and first + sub - 1 > MOBA_TOPK:
                    st = _nt(jnp.concatenate([km16[u, j], kk], axis=0), qj)
                    bias = _moba_bias(st[:8] + st[8:16], qi)
                    st = st[16:]
                    blocks = [st[nb * TK:(nb + 1) * TK] + bias[nb:nb + 1, :] for nb in range(first)]
                    blocks.append(st[first * TK:] + jnp.where(_own_block(qi) > first, bias[first:first + 1, :], 0.0))
                    st = jnp.concatenate(blocks, axis=0)
                else:
                    st = _nt(kk, qj)
                st_last = _nt(k_ref[0, n - TK:n, c0:c0 + LANES], qj[TQ - TK:])
                v0 = (2 * u + j) * HALF
                ot = _causal_attend(st, st_last, vt_ref[v0:v0 + HALF, :n], causal, sub_max)
                outs.append(_normalized(ot, HALF))
            o = jnp.concatenate(outs, axis=0).T
            gate = _silu(g_ref[0, r0:r0 + TQ, u * LANES:(u + 1) * LANES].astype(F32))
            o_ref[0, r0:r0 + TQ, u * LANES:(u + 1) * LANES] = (o * gate).astype(BF16)


def _step_spec(s, width):
    return pl.BlockSpec((1, s, PAIRS * width), lambda bi, p: (bi, 0, p))


def _vt_spec(s, row0):
    rows = PAIRS * LANES
    return pl.BlockSpec((rows, s), lambda bi, p: (row0 // rows + p, bi))


def _attn_ab(q, k, vt, g, km, *, moba, sub_max):
    b, s, _ = q.shape
    in_specs = [_step_spec(s, 2 * LANES), _step_spec(s, 2 * LANES), _vt_spec(s, 0 if moba else 4 * LANES),
                _step_spec(s, LANES)]
    args = [q, k, vt, g]
    if moba:
        in_specs.append(pl.BlockSpec((1, 8, PAIRS * LANES), lambda bi, p: (bi, 0, p)))
        args.append(km)
    return pl.pallas_call(
        functools.partial(_attn_ab_kernel, moba=moba, nq=s // TQ, sub_max=sub_max),
        grid=(b, 4 // PAIRS),
        in_specs=in_specs,
        out_specs=_step_spec(s, LANES),
        out_shape=jax.ShapeDtypeStruct((b, s, 4 * LANES), BF16),
        compiler_params=_cparams(("arbitrary", "arbitrary")),
        name=("attn_moba" if moba else "attn_fox") + ("_submax" if sub_max else ""),
    )(*args)


def _attn_diff_kernel(q_ref, k_ref, vt_ref, g_ref, lam_ref, sg_ref, o_ref, *, lam_init, nq, sub_max):
    lp = lam_ref[...]
    s1 = jnp.sum(lp[0:1] * lp[1:2], axis=-1, keepdims=True)
    s2 = jnp.sum(lp[2:3] * lp[3:4], axis=-1, keepdims=True)
    lam = jnp.exp(s1) - jnp.exp(s2) + lam_init
    causal = _causal_mask_t()
    for qi in range(nq):
        r0, n = qi * TQ, (qi + 1) * TQ
        for u in range(PAIRS):
            outs = []
            for j in (0, 1):
                c0 = (2 * u + j) * LANES
                qj = q_ref[0, r0:r0 + TQ, c0:c0 + LANES]
                st = _nt(k_ref[0, :n - TK, c0:c0 + LANES], qj)
                st_last = _nt(k_ref[0, n - TK:n, c0:c0 + LANES], qj[TQ - TK:])
                ot = _causal_attend(st, st_last, vt_ref[u * LANES:(u + 1) * LANES, :n], causal, sub_max)
                outs.append(_normalized(ot, LANES))
            oc = (outs[0] - lam * outs[1]).T
            oc = _rmsnorm_rows(oc, sg_ref[...]) * (1.0 - lam_init)
            gate = _silu(g_ref[0, r0:r0 + TQ, u * LANES:(u + 1) * LANES].astype(F32))
            o_ref[0, r0:r0 + TQ, u * LANES:(u + 1) * LANES] = (oc * gate).astype(BF16)


def _attn_diff(q, k, vt, g, lam, sg, *, lam_init, sub_max):
    b, s, _ = q.shape
    return pl.pallas_call(
        functools.partial(_attn_diff_kernel, lam_init=lam_init, nq=s // TQ, sub_max=sub_max),
        grid=(b, 4 // PAIRS),
        in_specs=[_step_spec(s, 2 * LANES), _step_spec(s, 2 * LANES), _vt_spec(s, 0),
                  _step_spec(s, LANES),
                  pl.BlockSpec(lam.shape, lambda bi, p: (0, 0)),
                  pl.BlockSpec(sg.shape, lambda bi, p: (0, 0))],
        out_specs=_step_spec(s, LANES),
        out_shape=jax.ShapeDtypeStruct((b, s, 4 * LANES), BF16),
        compiler_params=_cparams(("arbitrary", "arbitrary")),
        name="attn_diff" + ("_submax" if sub_max else ""),
    )(q, k, vt, g, lam, sg)


def _attn_swa_kernel(q_ref, k_ref, vt_ref, g_ref, sink_ref, o_ref, *, nt, sub_max):
    w = SWA_BLOCK
    key = lax.broadcasted_iota(jnp.int32, (w, w), 0)
    qry = lax.broadcasted_iota(jnp.int32, (w, w), 1)
    bias = {0: jnp.where(key <= qry, 0.0, MASK_NEG),
            1: jnp.where(key > qry, 0.0, MASK_NEG)}
    dark = jnp.full((w, w), MASK_NEG, F32)
    for ti in range(nt):
        t0 = ti * TS
        start = max(t0 - w, 0)
        nkeys = t0 + TS - start
        heads = []
        for j in (0, 1):
            qs = jnp.concatenate([q_ref[0, t0:t0 + TS, (2 * p + j) * LANES:(2 * p + j + 1) * LANES]
                                  for p in range(4)], axis=0)
            st = _nt(k_ref[0, start:start + nkeys, j * LANES:(j + 1) * LANES], qs)
            sink = sink_ref[j:j + 1, :]

            def blocks(fn):
                rows = []
                for r in range(nkeys // w):
                    cols = []
                    for c in range(4 * TS // w):
                        dist = (t0 // w + c % (TS // w)) - (start // w + r)
                        cols.append(fn(st[r * w:(r + 1) * w, c * w:(c + 1) * w], dist))
                    rows.append(jnp.concatenate(cols, axis=1))
                return jnp.concatenate(rows, axis=0)

            if sub_max:
                st = blocks(lambda blk, d: blk + bias.get(d, dark))
                m = jnp.maximum(jnp.max(st, axis=0, keepdims=True), sink)
                pt, sink = jnp.exp(st - m).astype(BF16), sink - m
            else:
                pt = blocks(lambda blk, d: jnp.exp(blk + bias[d]).astype(BF16) if d in bias
                            else jnp.zeros((w, w), BF16))
            ot = _attend(pt, vt_ref[j * HALF:(j + 1) * HALF, start:start + nkeys])
            heads.append(ot[:HALF] / (ot[HALF:HALF + 1] + jnp.exp(sink)))
        for p in range(4):
            o = jnp.concatenate([hd[:, p * TS:(p + 1) * TS] for hd in heads], axis=0).T
            gate = _silu(g_ref[0, t0:t0 + TS, p * LANES:(p + 1) * LANES].astype(F32))
            o_ref[0, t0:t0 + TS, p * LANES:(p + 1) * LANES] = (o * gate).astype(BF16)


def _attn_swa(q, k, vt, g, sinks, *, sub_max):
    b, s, _ = q.shape
    whole = lambda width: pl.BlockSpec((1, s, width), lambda bi: (bi, 0, 0))
    return pl.pallas_call(
        functools.partial(_attn_swa_kernel, nt=s // TS, sub_max=sub_max),
        grid=(b,),
        in_specs=[whole(8 * LANES), whole(2 * LANES), pl.BlockSpec((LANES, s), lambda bi: (4, bi)),
                  whole(4 * LANES), pl.BlockSpec(sinks.shape, lambda bi: (0, 0))],
        out_specs=whole(4 * LANES),
        out_shape=jax.ShapeDtypeStruct((b, s, 4 * LANES), BF16),
        compiler_params=_cparams(("arbitrary",)),
        name="attn_swa" + ("_submax" if sub_max else ""),
    )(q, k, vt, g, sinks)


SWA_HEAD_ORDER = (0, 4, 1, 5, 2, 6, 3, 7)


def _tile_gain(g, reps):
    return jnp.tile(g.astype(F32), reps)[None, :]


def _logit_bound(gq, gk):
    return HEAD_DIM * SCALE * jnp.max(jnp.abs(gq)) * jnp.max(jnp.abs(gk))


def _guarded(bound, fn, *args):
    return lax.cond(bound <= LOGIT_SAFE, functools.partial(fn, sub_max=False),
                    functools.partial(fn, sub_max=True), *args)


def kernel(x, norm_gain, w_in_ab, b_forget, moba_q_gain, moba_k_gain, fox_q_gain, fox_k_gain, w_out_ab,
           w_in_cd, diff_q_gain, diff_k_gain, diff_lambda, diff_subln_gain, swa_q_gain, swa_k_gain,
           swa_sinks, w_out_cd):
    b, s, d = x.shape
    assert d == 1024 and s % TM == 0 and s // MOBA_BLOCK == 8
    depth = norm_gain.shape[0]
    x2 = x.reshape(b * s, d)

    grp = np.arange(256) // HEAD_DIM
    e256 = jnp.asarray(grp[:, None] == grp[None, :], dtype=BF16)
    tri = jnp.asarray(np.triu(np.ones((TM, TM), np.float32)), dtype=BF16)
    perm = np.concatenate([np.arange(HEAD_DIM) + HEAD_DIM * hh for hh in SWA_HEAD_ORDER])
    r3 = lambda a: a.reshape(b, s, a.shape[1])

    res = None
    for layer in range(depth):
        j = layer // 2
        ng = norm_gain[layer][None, :]
        if layer % 2 == 0:
            wj = w_in_ab[j]
            w = jnp.concatenate([wj[:, 0:1024], wj[:, 1536:2048], wj[:, 2048:3072], wj[:, 3584:4096]],
                                axis=1).astype(BF16)
            nf = wj.shape[1] - 4096
            wvt = jnp.concatenate([wj[:, 1024:1536], wj[:, 3072:3584],
                                   jnp.pad(wj[:, 4096:], ((0, 0), (0, BF16_ROWS - nf)))], axis=1).T.astype(BF16)
            bf = jnp.pad(b_forget[j].astype(F32), (0, BF16_ROWS - nf))[:, None]
            outs = _in_ab(
                x2, res, ng, w, wvt, _tile_gain(moba_q_gain[j], 8), _tile_gain(moba_k_gain[j], 8),
                _tile_gain(fox_q_gain[j], 8), _tile_gain(fox_k_gain[j], 8), bf, e256, tri, seq=s)
            if res is not None:
                x2, outs = outs[0], outs[1:]
            qa, ka, ga, qb, kb, gb, vt, km = outs
            ya = _guarded(_logit_bound(moba_q_gain[j], moba_k_gain[j]), functools.partial(_attn_ab, moba=True),
                          r3(qa), r3(ka), vt, r3(ga), km.reshape(b, 8, 512))
            yb = _guarded(_logit_bound(fox_q_gain[j], fox_k_gain[j]), functools.partial(_attn_ab, moba=False),
                          r3(qb), r3(kb), vt, r3(gb), None)
            res = (ya.reshape(b * s, 512), yb.reshape(b * s, 512), w_out_ab[j].astype(BF16))
        else:
            wj = w_in_cd[j]
            w = jnp.concatenate([wj[:, 0:1024], wj[:, 1536:2048], wj[:, 2048:2560][:, perm],
                                 wj[:, 2560:2688], wj[:, 2816:3328][:, perm]], axis=1).astype(BF16)
            wvt = jnp.concatenate([wj[:, 1024:1536], wj[:, 2688:2816]], axis=1).T.astype(BF16)
            outs = _in_cd(
                x2, res, ng, w, wvt, _tile_gain(diff_q_gain[j], 8), _tile_gain(diff_k_gain[j], 8),
                _tile_gain(swa_q_gain[j], 8), _tile_gain(swa_k_gain[j], 2), e256, seq=s)
            if res is not None:
                x2, outs = outs[0], outs[1:]
            qc, kc, gc, qd, kd, gd, vt = outs
            lam_init = 0.8 - 0.6 * math.exp(-0.3 * layer)
            yc = _guarded(_logit_bound(diff_q_gain[j], diff_k_gain[j]),
                          functools.partial(_attn_diff, lam_init=lam_init),
                          r3(qc), r3(kc), vt, r3(gc), diff_lambda[j].astype(F32),
                          diff_subln_gain[j].astype(F32)[None, :])
            sinks = swa_sinks[j].astype(F32).reshape(2, 4)
            swa_bound = jnp.maximum(_logit_bound(swa_q_gain[j], swa_k_gain[j]), jnp.max(sinks))
            yd = _guarded(swa_bound, _attn_swa,
                          r3(qd), r3(kd), vt, r3(gd), jnp.repeat(sinks, TS, axis=1))
            wo = w_out_cd[j]
            wo = jnp.concatenate([wo[:512], wo[512:][perm]], axis=0).astype(BF16)
            res = (yc.reshape(b * s, 512), yd.reshape(b * s, 512), wo)
    return _out_proj(x2, *res).reshape(b, s, d)
```

```python
import functools
import math

import numpy as np
import jax
import jax.numpy as jnp
from jax import lax
from jax.experimental import pallas as pl
from jax.experimental.pallas import tpu as pltpu

F32 = jnp.float32
BF16 = jnp.bfloat16

LANES = 128
BF16_ROWS = 16
HEAD_DIM = 64
HALF = 64
EPS = 1e-6
SCALE = HEAD_DIM ** -0.5
TM = 512
TM_OUT = 1024
TQ = 512
TK = 256
PAIRS = 2
TS = 256
SWA_BLOCK = 128
MOBA_BLOCK = 256
MOBA_TOPK = 3
MASK_NEG = -1e30
SEL_NEG = -1e9
LOGIT_SAFE = 32.0
VMEM_LIMIT = 56 * 1024 * 1024

F_ALIBI = 0
F_FOX = 0


def _alibi_slopes(n):
    s = [2.0 ** (-8.0 * (i + 1) / n) for i in range(n)]
    for v in s:
        m, _ = math.frexp(v)
        assert m == 0.5, "feature-lane ALiBi needs power-of-two slopes"
    return s


def _cparams(sem):
    return pltpu.CompilerParams(dimension_semantics=sem, vmem_limit_bytes=VMEM_LIMIT)


def _nt(a, b):
    return lax.dot_general(a, b, (((1,), (1,)), ((), ())), preferred_element_type=F32)


def _group_sums(sq, e_ref):
    width = sq.shape[1]
    if width == LANES:
        return jnp.dot(sq.astype(BF16), e_ref[:LANES, :LANES], preferred_element_type=F32)
    e = e_ref[...]
    outs = [jnp.dot(sq[:, c:c + 256].astype(BF16), e, preferred_element_type=F32)
            for c in range(0, width, 256)]
    return outs[0] if len(outs) == 1 else jnp.concatenate(outs, axis=1)


def _head_norm(p, gain_row, e_ref):
    ssq = _group_sums(p * p, e_ref)
    return p * lax.rsqrt(ssq * (1.0 / HEAD_DIM) + EPS) * gain_row


def _lane_iota(rows):
    return lax.broadcasted_iota(jnp.int32, (rows, LANES), 1)


def _pos_features(i, tiles_per_seq):
    lane = _lane_iota(TM)
    pos = (i % tiles_per_seq) * TM + lax.broadcasted_iota(jnp.int32, (TM, LANES), 0)
    a64 = ((pos >> 6) << 6).astype(F32)
    a = (pos >> 6).astype(F32)
    b = (pos & 63).astype(F32)
    out = []
    for j in (0, 1):
        base = (1 - j) * HALF + F_ALIBI
        qb = jnp.where(lane == base, -a64,
             jnp.where(lane == base + 1, -b,
             jnp.where(lane == base + 2, 64.0,
             jnp.where(lane == base + 3, 1.0, 0.0))))
        kf = jnp.where((lane == base) | (lane == base + 1), 1.0,
             jnp.where(lane == base + 2, a,
             jnp.where(lane == base + 3, b, 0.0)))
        out.append((qb, kf))
    return lane, out


def _rmsnorm_rows(x, gain_row):
    return x * lax.rsqrt(jnp.mean(x * x, axis=-1, keepdims=True) + EPS) * gain_row


def _residual_tile(refs, fused, n_in):
    if not fused:
        return refs[0][...], refs[1:]
    x_ref, ya_ref, yb_ref, wo_ref = refs[:4]
    rest = refs[4:]
    half = ya_ref.shape[1]
    x = (x_ref[...]
         + jnp.dot(ya_ref[...], wo_ref[:half, :], preferred_element_type=F32)
         + jnp.dot(yb_ref[...], wo_ref[half:, :], preferred_element_type=F32))
    rest[n_in][...] = x
    return x, rest[:n_in] + rest[n_in + 1:]


def _whole(a, layer=None):
    if layer is None:
        return pl.BlockSpec(a.shape, lambda i: (0,) * a.ndim)
    return pl.BlockSpec((None,) + a.shape[1:], lambda i: (layer,) + (0,) * (a.ndim - 1))


def _residual_specs(x2, res):
    row = lambda width: pl.BlockSpec((TM, width), lambda i: (i, 0))
    if res is None:
        return [x2], [row(x2.shape[1])], [], []
    ya, yb, wo, wo_layer = res
    return ([x2, ya, yb, wo],
            [row(x2.shape[1]), row(ya.shape[1]), row(yb.shape[1]), _whole(wo, wo_layer)],
            [row(x2.shape[1])], [jax.ShapeDtypeStruct(x2.shape, F32)])


def _store_blocks(dst_ref, src, feats):
    lane = _lane_iota(src.shape[0])
    for p in range(src.shape[1] // LANES):
        pair = src[:, p * LANES:(p + 1) * LANES]
        for j in (0, 1):
            data = (lane >= HALF) if j == 1 else (lane < HALF)
            blk = jnp.where(data, pair, feats(p, j))
            dst_ref[:, (2 * p + j) * LANES:(2 * p + j + 1) * LANES] = blk.astype(BF16)


def _in_ab_kernel(*refs, tiles_per_seq, fused):
    x, refs = _residual_tile(refs, fused, 10)
    (ng_ref, w_ref, wvt_ref, gqa_ref, gka_ref, gqb_ref, gkb_ref, bf_ref, e_ref, tri_ref,
     qa_ref, ka_ref, ga_ref, qb_ref, kb_ref, gb_ref, vt_ref, km_ref, carry_ref) = refs
    i = pl.program_id(0)

    @pl.when(i % tiles_per_seq == 0)
    def _():
        carry_ref[...] = jnp.zeros_like(carry_ref)

    h = _rmsnorm_rows(x, ng_ref[...]).astype(BF16)

    def proj(c0, width):
        return jnp.dot(h, w_ref[:, c0:c0 + width], preferred_element_type=F32)

    lane, pf = _pos_features(i, tiles_per_seq)
    slopes = _alibi_slopes(8)
    rows_t = _nt(wvt_ref[...], h)
    nv = vt_ref.shape[0]
    vt_ref[...] = rows_t[:nv].astype(BF16)

    qa = _head_norm(proj(0, 512), gqa_ref[...] * SCALE, e_ref)
    _store_blocks(qa_ref, qa, lambda p, j: pf[j][0] * slopes[2 * p + j])
    ka = _head_norm(proj(512, 512), gka_ref[...], e_ref)
    _store_blocks(ka_ref, ka, lambda p, j: pf[j][1])
    km = [jnp.mean(ka[r * MOBA_BLOCK:(r + 1) * MOBA_BLOCK], axis=0, keepdims=True)
          for r in range(TM // MOBA_BLOCK)]
    km_ref[0] = jnp.concatenate(km, axis=0)
    ga_ref[...] = proj(1024, 512).astype(BF16)

    z = rows_t[nv:] + bf_ref[...]
    logf = jnp.minimum(z, 0.0) - jnp.log1p(jnp.exp(-jnp.abs(z)))
    hi = logf.astype(BF16)
    r1 = logf - hi.astype(F32)
    mid = r1.astype(BF16)
    lo = (r1 - mid.astype(F32)).astype(BF16)
    cs = jnp.dot(jnp.concatenate([hi, mid, lo], axis=0), tri_ref[...], preferred_element_type=F32)
    nh = logf.shape[0]
    ct = cs[:nh] + cs[nh:2 * nh] + cs[2 * nh:] + carry_ref[:, 0:1]
    carry_ref[...] = jnp.broadcast_to(ct[:, TM - 1:TM], carry_ref.shape)
    c = jnp.concatenate([ct, jnp.zeros((LANES - nh, TM), F32)], axis=0).T
    chi = c.astype(BF16).astype(F32)
    r1 = c - chi
    cmid = r1.astype(BF16).astype(F32)
    clo = r1 - cmid

    def bcast(arr, hh):
        return jnp.broadcast_to(arr[:, hh:hh + 1], (TM, LANES))

    def fox_q(p, j):
        hh, base = 2 * p + j, (1 - j) * HALF + F_FOX
        ones = jnp.where((lane >= base + 3) & (lane < base + 6), 1.0, 0.0)
        return jnp.where(lane == base, bcast(chi, hh),
               jnp.where(lane == base + 1, bcast(cmid, hh),
               jnp.where(lane == base + 2, bcast(clo, hh), ones)))

    def fox_k(p, j):
        hh, base = 2 * p + j, (1 - j) * HALF + F_FOX
        ones = jnp.where((lane >= base) & (lane < base + 3), 1.0, 0.0)
        return jnp.where(lane == base + 3, -bcast(chi, hh),
               jnp.where(lane == base + 4, -bcast(cmid, hh),
               jnp.where(lane == base + 5, -bcast(clo, hh), ones)))

    qb = _head_norm(proj(1536, 512), gqb_ref[...] * SCALE, e_ref)
    _store_blocks(qb_ref, qb, fox_q)
    kb = _head_norm(proj(2048, 512), gkb_ref[...], e_ref)
    _store_blocks(kb_ref, kb, fox_k)
    gb_ref[...] = proj(2560, 512).astype(BF16)


def _in_ab(x2, res, ng, w, wvt, gqa, gka, gqb, gkb, bf, e256, tri, *, seq, layer):
    n = x2.shape[0]
    nt = n // TM
    row = lambda width: pl.BlockSpec((TM, width), lambda i: (i, 0))
    bf16 = lambda width: jax.ShapeDtypeStruct((n, width), BF16)
    res_in, res_in_specs, res_out_specs, res_out_shapes = _residual_specs(x2, res)
    consts = [ng, w, wvt, gqa, gka, gqb, gkb, bf, e256, tri]
    return pl.pallas_call(
        functools.partial(_in_ab_kernel, tiles_per_seq=seq // TM, fused=res is not None),
        grid=(nt,),
        in_specs=(res_in_specs + [_whole(ng), _whole(w, layer), _whole(wvt, layer)]
                  + [_whole(a) for a in consts[3:]]),
        out_specs=res_out_specs + [row(1024), row(1024), row(512), row(1024), row(1024), row(512),
                                   pl.BlockSpec((wvt.shape[1] - BF16_ROWS, TM), lambda i: (0, i)),
                                   pl.BlockSpec((1, TM // MOBA_BLOCK, 512), lambda i: (i, 0, 0))],
        out_shape=res_out_shapes + [bf16(1024), bf16(1024), bf16(512), bf16(1024), bf16(1024), bf16(512),
                                    jax.ShapeDtypeStruct((wvt.shape[1] - BF16_ROWS, n), BF16),
                                    jax.ShapeDtypeStruct((nt, TM // MOBA_BLOCK, 512), F32)],
        scratch_shapes=[pltpu.VMEM((BF16_ROWS, LANES), F32)],
        compiler_params=_cparams(("arbitrary",)),
        name="in_proj_ab",
    )(*res_in, *consts)


def _in_cd_kernel(*refs, tiles_per_seq, fused):
    x, refs = _residual_tile(refs, fused, 8)
    (ng_ref, w_ref, wvt_ref, gqc_ref, gkc_ref, gqd_ref, gkd_ref, e_ref,
     qc_ref, kc_ref, gc_ref, qd_ref, kd_ref, gd_ref, vt_ref) = refs
    i = pl.program_id(0)
    h = _rmsnorm_rows(x, ng_ref[...]).astype(BF16)

    def proj(c0, width):
        return jnp.dot(h, w_ref[:, c0:c0 + width], preferred_element_type=F32)

    _, pf = _pos_features(i, tiles_per_seq)
    dslopes = _alibi_slopes(4)
    sslopes = _alibi_slopes(8)
    vt_ref[...] = _nt(wvt_ref[...], h).astype(BF16)

    qc = _head_norm(proj(0, 512), gqc_ref[...] * SCALE, e_ref)
    _store_blocks(qc_ref, qc, lambda p, j: pf[j][0] * dslopes[p])
    kc = _head_norm(proj(512, 512), gkc_ref[...], e_ref)
    _store_blocks(kc_ref, kc, lambda p, j: pf[j][1])
    gc_ref[...] = proj(1024, 512).astype(BF16)

    qd = _head_norm(proj(1536, 512), gqd_ref[...] * SCALE, e_ref)
    _store_blocks(qd_ref, qd, lambda p, j: pf[j][0] * sslopes[p + 4 * j])
    kd = _head_norm(proj(2048, LANES), gkd_ref[...], e_ref)
    _store_blocks(kd_ref, kd, lambda p, j: pf[j][1])
    gd_ref[...] = proj(2176, 512).astype(BF16)


def _in_cd(x2, res, ng, w, wvt, gqc, gkc, gqd, gkd, e256, *, seq, layer):
    n = x2.shape[0]
    nt = n // TM
    row = lambda width: pl.BlockSpec((TM, width), lambda i: (i, 0))
    bf16 = lambda width: jax.ShapeDtypeStruct((n, width), BF16)
    res_in, res_in_specs, res_out_specs, res_out_shapes = _residual_specs(x2, res)
    consts = [ng, w, wvt, gqc, gkc, gqd, gkd, e256]
    return pl.pallas_call(
        functools.partial(_in_cd_kernel, tiles_per_seq=seq // TM, fused=res is not None),
        grid=(nt,),
        in_specs=(res_in_specs + [_whole(ng), _whole(w, layer), _whole(wvt, layer)]
                  + [_whole(a) for a in consts[3:]]),
        out_specs=res_out_specs + [row(1024), row(1024), row(512), row(1024), row(256), row(512),
                                   pl.BlockSpec((wvt.shape[1], TM), lambda i: (0, i))],
        out_shape=res_out_shapes + [bf16(1024), bf16(1024), bf16(512), bf16(1024), bf16(256), bf16(512),
                                    jax.ShapeDtypeStruct((wvt.shape[1], n), BF16)],
        compiler_params=_cparams(("arbitrary",)),
        name="in_proj_cd",
    )(*res_in, *consts)


def _out_kernel(x_ref, ya_ref, yb_ref, w_ref, o_ref):
    half = ya_ref.shape[1]
    o_ref[...] = (x_ref[...]
                  + jnp.dot(ya_ref[...], w_ref[:half, :], preferred_element_type=F32)
                  + jnp.dot(yb_ref[...], w_ref[half:, :], preferred_element_type=F32))


def _out_proj(x2, ya, yb, w, layer):
    n, d = x2.shape
    row = lambda width: pl.BlockSpec((TM_OUT, width), lambda i: (i, 0))
    return pl.pallas_call(
        _out_kernel,
        grid=(n // TM_OUT,),
        in_specs=[row(d), row(ya.shape[1]), row(yb.shape[1]), _whole(w, layer)],
        out_specs=row(d),
        out_shape=jax.ShapeDtypeStruct((n, d), F32),
        compiler_params=_cparams(("arbitrary",)),
        name="out_proj",
    )(x2, ya, yb, w)


def _silu(g):
    return g * (1.0 / (1.0 + jnp.exp(-g)))


def _attend(pt, vt):
    lhs = jnp.concatenate([vt, jnp.ones((BF16_ROWS, vt.shape[1]), BF16)], axis=0)
    return jnp.dot(lhs, pt, preferred_element_type=F32)


def _causal_attend(st_main, st_last, vt, causal, sub_max):
    assert TQ == 2 * TK
    nm = st_main.shape[0]
    tri = causal[:TK, :TK]
    if sub_max:
        dark = jnp.full((TK, TQ - TK), MASK_NEG, F32)
        st = jnp.concatenate([st_main, jnp.concatenate([dark, st_last], axis=1)], axis=0)
        tail = jnp.where(causal, st[nm - TK:], MASK_NEG)
        st = tail if nm == TK else jnp.concatenate([st[:nm - TK], tail], axis=0)
        return _attend(jnp.exp(st - jnp.max(st, axis=0, keepdims=True)).astype(BF16), vt)
    own = st_main[nm - TK:]
    own = jnp.concatenate([jnp.where(tri, own[:, :TK], MASK_NEG), own[:, TK:]], axis=1)
    parts = ([st_main[:nm - TK]] if nm > TK else []) + [own]
    ot = _attend(jnp.concatenate([jnp.exp(p).astype(BF16) for p in parts], axis=0), vt[:, :nm])
    ot_last = _attend(jnp.exp(jnp.where(tri, st_last, MASK_NEG)).astype(BF16), vt[:, nm:])
    return jnp.concatenate([ot[:, :TK], ot[:, TK:] + ot_last], axis=1)


def _normalized(ot, dims):
    return ot[:dims] / ot[dims:dims + 1]


def _causal_mask_t():
    key = lax.broadcasted_iota(jnp.int32, (TQ, TQ), 0)
    qry = lax.broadcasted_iota(jnp.int32, (TQ, TQ), 1)
    return key <= qry


def _own_block(qi):
    return qi * (TQ // TK) + lax.broadcasted_iota(jnp.int32, (1, TQ), 1) // TK


def _moba_bias(gate, qi):
    first = qi * (TQ // TK)
    own = _own_block(qi)
    n_iota = lax.broadcasted_iota(jnp.int32, (8, TQ), 0)
    cnt = jnp.zeros((8, TQ), F32)
    for n2 in range(first + TQ // TK - 1):
        g2 = jnp.broadcast_to(gate[n2:n2 + 1, :], (8, TQ))
        beats = (g2 > gate) | ((g2 == gate) & (n_iota > n2))
        if n2 >= first:
            beats = beats & (own > n2)
        cnt = cnt + jnp.where(beats, 1.0, 0.0)
    return jnp.where(cnt < MOBA_TOPK, 0.0, SEL_NEG)


def _attn_ab_kernel(*refs, moba, nq, sub_max):
    if moba:
        q_ref, k_ref, vt_ref, g_ref, km_ref, o_ref = refs
    else:
        q_ref, k_ref, vt_ref, g_ref, o_ref = refs
    causal = _causal_mask_t()
    km16 = {}
    if moba:
        lane8 = lax.broadcasted_iota(jnp.int32, (8, LANES), 1)
        for u in range(PAIRS):
            for j in (0, 1):
                km = km_ref[0, :, u * LANES:(u + 1) * LANES]
                km = jnp.where((lane8 >= HALF) if j == 1 else (lane8 < HALF), km, 0.0)
                hi = km.astype(BF16)
                lo = (km - hi.astype(F32)).astype(BF16)
                km16[u, j] = jnp.concatenate([hi, lo], axis=0)
    sub = TQ // TK
    for qi in range(nq):
        r0, n = qi * TQ, (qi + 1) * TQ
        first = qi * sub
        for u in range(PAIRS):
            outs = []
            for j in (0, 1):
                c0 = (2 * u + j) * LANES
                qj = q_ref[0, r0:r0 + TQ, c0:c0 + LANES]
                kk = k_ref[0, :n - TK, c0:c0 + LANES]
                if moba and first + sub - 1 > MOBA_TOPK:
                    st = _nt(jnp.concatenate([km16[u, j], kk], axis=0), qj)
                    bias = _moba_bias(st[:8] + st[8:16], qi)
                    st = st[16:]
                    blocks = [st[nb * TK:(nb + 1) * TK] + bias[nb:nb + 1, :] for nb in range(first)]
                    blocks.append(st[first * TK:] + jnp.where(_own_block(qi) > first, bias[first:first + 1, :], 0.0))
                    st = jnp.concatenate(blocks, axis=0)
                else:
                    st = _nt(kk, qj)
                st_last = _nt(k_ref[0, n - TK:n, c0:c0 + LANES], qj[TQ - TK:])
                v0 = (2 * u + j) * HALF
                ot = _causal_attend(st, st_last, vt_ref[v0:v0 + HALF, :n], causal, sub_max)
                outs.append(_normalized(ot, HALF))
            o = jnp.concatenate(outs, axis=0).T
            gate = _silu(g_ref[0, r0:r0 + TQ, u * LANES:(u + 1) * LANES].astype(F32))
            o_ref[0, r0:r0 + TQ, u * LANES:(u + 1) * LANES] = (o * gate).astype(BF16)


def _step_spec(s, width):
    return pl.BlockSpec((1, s, PAIRS * width), lambda bi, p: (bi, 0, p))


def _vt_spec(s, row0):
    rows = PAIRS * LANES
    return pl.BlockSpec((rows, s), lambda bi, p: (row0 // rows + p, bi))


def _attn_ab(q, k, vt, g, km, *, moba, sub_max):
    b, s, _ = q.shape
    in_specs = [_step_spec(s, 2 * LANES), _step_spec(s, 2 * LANES), _vt_spec(s, 0 if moba else 4 * LANES),
                _step_spec(s, LANES)]
    args = [q, k, vt, g]
    if moba:
        in_specs.append(pl.BlockSpec((1, 8, PAIRS * LANES), lambda bi, p: (bi, 0, p)))
        args.append(km)
    return pl.pallas_call(
        functools.partial(_attn_ab_kernel, moba=moba, nq=s // TQ, sub_max=sub_max),
        grid=(b, 4 // PAIRS),
        in_specs=in_specs,
        out_specs=_step_spec(s, LANES),
        out_shape=jax.ShapeDtypeStruct((b, s, 4 * LANES), BF16),
        compiler_params=_cparams(("arbitrary", "arbitrary")),
        name=("attn_moba" if moba else "attn_fox") + ("_submax" if sub_max else ""),
    )(*args)


def _attn_diff_kernel(q_ref, k_ref, vt_ref, g_ref, lam_ref, sg_ref, o_ref, *, lam_init, nq, sub_max):
    lp = lam_ref[...]
    s1 = jnp.sum(lp[0:1] * lp[1:2], axis=-1, keepdims=True)
    s2 = jnp.sum(lp[2:3] * lp[3:4], axis=-1, keepdims=True)
    lam = jnp.exp(s1) - jnp.exp(s2) + lam_init
    causal = _causal_mask_t()
    for qi in range(nq):
        r0, n = qi * TQ, (qi + 1) * TQ
        for u in range(PAIRS):
            outs = []
            for j in (0, 1):
                c0 = (2 * u + j) * LANES
                qj = q_ref[0, r0:r0 + TQ, c0:c0 + LANES]
                st = _nt(k_ref[0, :n - TK, c0:c0 + LANES], qj)
                st_last = _nt(k_ref[0, n - TK:n, c0:c0 + LANES], qj[TQ - TK:])
                ot = _causal_attend(st, st_last, vt_ref[u * LANES:(u + 1) * LANES, :n], causal, sub_max)
                outs.append(_normalized(ot, LANES))
            oc = (outs[0] - lam * outs[1]).T
            oc = _rmsnorm_rows(oc, sg_ref[...]) * (1.0 - lam_init)
            gate = _silu(g_ref[0, r0:r0 + TQ, u * LANES:(u + 1) * LANES].astype(F32))
            o_ref[0, r0:r0 + TQ, u * LANES:(u + 1) * LANES] = (oc * gate).astype(BF16)


def _attn_diff(q, k, vt, g, lam, sg, *, lam_init, sub_max):
    b, s, _ = q.shape
    return pl.pallas_call(
        functools.partial(_attn_diff_kernel, lam_init=lam_init, nq=s // TQ, sub_max=sub_max),
        grid=(b, 4 // PAIRS),
        in_specs=[_step_spec(s, 2 * LANES), _step_spec(s, 2 * LANES), _vt_spec(s, 0),
                  _step_spec(s, LANES),
                  pl.BlockSpec(lam.shape, lambda bi, p: (0, 0)),
                  pl.BlockSpec(sg.shape, lambda bi, p: (0, 0))],
        out_specs=_step_spec(s, LANES),
        out_shape=jax.ShapeDtypeStruct((b, s, 4 * LANES), BF16),
        compiler_params=_cparams(("arbitrary", "arbitrary")),
        name="attn_diff" + ("_submax" if sub_max else ""),
    )(q, k, vt, g, lam, sg)


def _attn_swa_kernel(q_ref, k_ref, vt_ref, g_ref, sink_ref, o_ref, *, nt, sub_max):
    w = SWA_BLOCK
    key = lax.broadcasted_iota(jnp.int32, (w, w), 0)
    qry = lax.broadcasted_iota(jnp.int32, (w, w), 1)
    bias = {0: jnp.where(key <= qry, 0.0, MASK_NEG),
            1: jnp.where(key > qry, 0.0, MASK_NEG)}
    dark = jnp.full((w, w), MASK_NEG, F32)
    for ti in range(nt):
        t0 = ti * TS
        start = max(t0 - w, 0)
        nkeys = t0 + TS - start
        heads = []
        for j in (0, 1):
            qs = jnp.concatenate([q_ref[0, t0:t0 + TS, (2 * p + j) * LANES:(2 * p + j + 1) * LANES]
                                  for p in range(4)], axis=0)
            st = _nt(k_ref[0, start:start + nkeys, j * LANES:(j + 1) * LANES], qs)
            sink = sink_ref[j:j + 1, :]

            def blocks(fn):
                rows = []
                for r in range(nkeys // w):
                    cols = []
                    for c in range(4 * TS // w):
                        dist = (t0 // w + c % (TS // w)) - (start // w + r)
                        cols.append(fn(st[r * w:(r + 1) * w, c * w:(c + 1) * w], dist))
                    rows.append(jnp.concatenate(cols, axis=1))
                return jnp.concatenate(rows, axis=0)

            if sub_max:
                st = blocks(lambda blk, d: blk + bias.get(d, dark))
                m = jnp.maximum(jnp.max(st, axis=0, keepdims=True), sink)
                pt, sink = jnp.exp(st - m).astype(BF16), sink - m
            else:
                pt = blocks(lambda blk, d: jnp.exp(blk + bias[d]).astype(BF16) if d in bias
                            else jnp.zeros((w, w), BF16))
            ot = _attend(pt, vt_ref[j * HALF:(j + 1) * HALF, start:start + nkeys])
            heads.append(ot[:HALF] / (ot[HALF:HALF + 1] + jnp.exp(sink)))
        for p in range(4):
            o = jnp.concatenate([hd[:, p * TS:(p + 1) * TS] for hd in heads], axis=0).T
            gate = _silu(g_ref[0, t0:t0 + TS, p * LANES:(p + 1) * LANES].astype(F32))
            o_ref[0, t0:t0 + TS, p * LANES:(p + 1) * LANES] = (o * gate).astype(BF16)


def _attn_swa(q, k, vt, g, sinks, *, sub_max):
    b, s, _ = q.shape
    whole = lambda width: pl.BlockSpec((1, s, width), lambda bi: (bi, 0, 0))
    return pl.pallas_call(
        functools.partial(_attn_swa_kernel, nt=s // TS, sub_max=sub_max),
        grid=(b,),
        in_specs=[whole(8 * LANES), whole(2 * LANES), pl.BlockSpec((LANES, s), lambda bi: (4, bi)),
                  whole(4 * LANES), pl.BlockSpec(sinks.shape, lambda bi: (0, 0))],
        out_specs=whole(4 * LANES),
        out_shape=jax.ShapeDtypeStruct((b, s, 4 * LANES), BF16),
        compiler_params=_cparams(("arbitrary",)),
        name="attn_swa" + ("_submax" if sub_max else ""),
    )(q, k, vt, g, sinks)


SWA_HEAD_ORDER = (0, 4, 1, 5, 2, 6, 3, 7)


def _tile_gain(g, reps):
    return jnp.tile(g.astype(F32), reps)[None, :]


def _logit_bound(gq, gk):
    return HEAD_DIM * SCALE * jnp.max(jnp.abs(gq)) * jnp.max(jnp.abs(gk))


def _guarded(bound, fn, *args):
    return lax.cond(bound <= LOGIT_SAFE, functools.partial(fn, sub_max=False),
                    functools.partial(fn, sub_max=True), *args)


def kernel(x, norm_gain, w_in_ab, b_forget, moba_q_gain, moba_k_gain, fox_q_gain, fox_k_gain, w_out_ab,
           w_in_cd, diff_q_gain, diff_k_gain, diff_lambda, diff_subln_gain, swa_q_gain, swa_k_gain,
           swa_sinks, w_out_cd):
    b, s, d = x.shape
    assert d == 1024 and s % TM == 0 and s // MOBA_BLOCK == 8
    depth = norm_gain.shape[0]
    x2 = x.reshape(b * s, d)

    grp = np.arange(256) // HEAD_DIM
    e256 = jnp.asarray(grp[:, None] == grp[None, :], dtype=BF16)
    tri = jnp.asarray(np.triu(np.ones((TM, TM), np.float32)), dtype=BF16)
    perm = np.concatenate([np.arange(HEAD_DIM) + HEAD_DIM * hh for hh in SWA_HEAD_ORDER])
    r3 = lambda a: a.reshape(b, s, a.shape[1])

    wab = w_in_ab
    nf = wab.shape[2] - 4096
    w_ab = jnp.concatenate([wab[:, :, 0:1024], wab[:, :, 1536:2048], wab[:, :, 2048:3072],
                            wab[:, :, 3584:4096]], axis=2).astype(BF16)
    wvt_ab = jnp.swapaxes(jnp.concatenate(
        [wab[:, :, 1024:1536], wab[:, :, 3072:3584],
         jnp.pad(wab[:, :, 4096:], ((0, 0), (0, 0), (0, BF16_ROWS - nf)))], axis=2), 1, 2).astype(BF16)
    wo_ab = w_out_ab.astype(BF16)
    wcd = w_in_cd
    w_cd = jnp.concatenate([wcd[:, :, 0:1024], wcd[:, :, 1536:2048], wcd[:, :, 2048:2560][:, :, perm],
                            wcd[:, :, 2560:2688], wcd[:, :, 2816:3328][:, :, perm]], axis=2).astype(BF16)
    wvt_cd = jnp.swapaxes(jnp.concatenate([wcd[:, :, 1024:1536], wcd[:, :, 2688:2816]], axis=2),
                          1, 2).astype(BF16)
    wo_cd = jnp.concatenate([w_out_cd[:, :512], w_out_cd[:, 512:][:, perm]], axis=1).astype(BF16)

    res = None
    for layer in range(depth):
        j = layer // 2
        ng = norm_gain[layer][None, :]
        if layer % 2 == 0:
            bf = jnp.pad(b_forget[j].astype(F32), (0, BF16_ROWS - nf))[:, None]
            outs = _in_ab(
                x2, res, ng, w_ab, wvt_ab, _tile_gain(moba_q_gain[j], 8), _tile_gain(moba_k_gain[j], 8),
                _tile_gain(fox_q_gain[j], 8), _tile_gain(fox_k_gain[j], 8), bf, e256, tri, seq=s, layer=j)
            if res is not None:
                x2, outs = outs[0], outs[1:]
            qa, ka, ga, qb, kb, gb, vt, km = outs
            ya = _guarded(_logit_bound(moba_q_gain[j], moba_k_gain[j]), functools.partial(_attn_ab, moba=True),
                          r3(qa), r3(ka), vt, r3(ga), km.reshape(b, 8, 512))
            yb = _guarded(_logit_bound(fox_q_gain[j], fox_k_gain[j]), functools.partial(_attn_ab, moba=False),
                          r3(qb), r3(kb), vt, r3(gb), None)
            res = (ya.reshape(b * s, 512), yb.reshape(b * s, 512), wo_ab, j)
        else:
            outs = _in_cd(
                x2, res, ng, w_cd, wvt_cd, _tile_gain(diff_q_gain[j], 8), _tile_gain(diff_k_gain[j], 8),
                _tile_gain(swa_q_gain[j], 8), _tile_gain(swa_k_gain[j], 2), e256, seq=s, layer=j)
            if res is not None:
                x2, outs = outs[0], outs[1:]
            qc, kc, gc, qd, kd, gd, vt = outs
            lam_init = 0.8 - 0.6 * math.exp(-0.3 * layer)
            yc = _guarded(_logit_bound(diff_q_gain[j], diff_k_gain[j]),
                          functools.partial(_attn_diff, lam_init=lam_init),
                          r3(qc), r3(kc), vt, r3(gc), diff_lambda[j].astype(F32),
                          diff_subln_gain[j].astype(F32)[None, :])
            sinks = swa_sinks[j].astype(F32).reshape(2, 4)
            swa_bound = jnp.maximum(_logit_bound(swa_q_gain[j], swa_k_gain[j]), jnp.max(sinks))
            yd = _guarded(swa_bound, _attn_swa,
                          r3(qd), r3(kd), vt, r3(gd), jnp.repeat(sinks, TS, axis=1))
            res = (yc.reshape(b * s, 512), yd.reshape(b * s, 512), wo_cd, j)
    return _out_proj(x2, *res).reshape(b, s, d)
```

```python
import functools
import math

import numpy as np
import jax
import jax.numpy as jnp
from jax import lax
from jax.experimental import pallas as pl
from jax.experimental.pallas import tpu as pltpu

F32 = jnp.float32
BF16 = jnp.bfloat16

LANES = 128
BF16_ROWS = 16
HEAD_DIM = 64
HALF = 64
EPS = 1e-6
SCALE = HEAD_DIM ** -0.5
TM = 512
TM_OUT = 2048
TQ = 512
TK = 256
PAIRS = 2
TS = 256
SWA_BLOCK = 128
MOBA_BLOCK = 256
MOBA_TOPK = 3
MASK_NEG = -1e30
SEL_NEG = -1e9
LOGIT_SAFE = 32.0
VMEM_LIMIT = 56 * 1024 * 1024

F_ALIBI = 0
F_FOX = 0


def _alibi_slopes(n):
    s = [2.0 ** (-8.0 * (i + 1) / n) for i in range(n)]
    for v in s:
        m, _ = math.frexp(v)
        assert m == 0.5, "feature-lane ALiBi needs power-of-two slopes"
    return s


def _cparams(sem):
    return pltpu.CompilerParams(dimension_semantics=sem, vmem_limit_bytes=VMEM_LIMIT)


def _nt(a, b):
    return lax.dot_general(a, b, (((1,), (1,)), ((), ())), preferred_element_type=F32)


def _group_sums(sq, e_ref):
    width = sq.shape[1]
    if width == LANES:
        return jnp.dot(sq.astype(BF16), e_ref[:LANES, :LANES], preferred_element_type=F32)
    e = e_ref[...]
    outs = [jnp.dot(sq[:, c:c + 256].astype(BF16), e, preferred_element_type=F32)
            for c in range(0, width, 256)]
    return outs[0] if len(outs) == 1 else jnp.concatenate(outs, axis=1)


def _head_norm(p, gain_row, e_ref):
    ssq = _group_sums(p * p, e_ref)
    return p * lax.rsqrt(ssq * (1.0 / HEAD_DIM) + EPS) * gain_row


def _lane_iota(rows):
    return lax.broadcasted_iota(jnp.int32, (rows, LANES), 1)


def _pos_features(i, tiles_per_seq):
    lane = _lane_iota(TM)
    pos = (i % tiles_per_seq) * TM + lax.broadcasted_iota(jnp.int32, (TM, LANES), 0)
    a64 = ((pos >> 6) << 6).astype(F32)
    a = (pos >> 6).astype(F32)
    b = (pos & 63).astype(F32)
    out = []
    for j in (0, 1):
        base = (1 - j) * HALF + F_ALIBI
        qb = jnp.where(lane == base, -a64,
             jnp.where(lane == base + 1, -b,
             jnp.where(lane == base + 2, 64.0,
             jnp.where(lane == base + 3, 1.0, 0.0))))
        kf = jnp.where((lane == base) | (lane == base + 1), 1.0,
             jnp.where(lane == base + 2, a,
             jnp.where(lane == base + 3, b, 0.0)))
        out.append((qb, kf))
    return lane, out


def _rmsnorm_rows(x, gain_row):
    return x * lax.rsqrt(jnp.mean(x * x, axis=-1, keepdims=True) + EPS) * gain_row


def _residual_tile(refs, fused, n_in):
    if not fused:
        return refs[0][...], refs[1:]
    x_ref, ya_ref, yb_ref, wo_ref = refs[:4]
    rest = refs[4:]
    half = ya_ref.shape[1]
    x = (x_ref[...]
         + jnp.dot(ya_ref[...], wo_ref[:half, :], preferred_element_type=F32)
         + jnp.dot(yb_ref[...], wo_ref[half:, :], preferred_element_type=F32))
    rest[n_in][...] = x
    return x, rest[:n_in] + rest[n_in + 1:]


def _residual_specs(x2, res):
    row = lambda width: pl.BlockSpec((TM, width), lambda i: (i, 0))
    if res is None:
        return [x2], [row(x2.shape[1])], [], []
    ya, yb, wo = res
    return ([x2, ya, yb, wo],
            [row(x2.shape[1]), row(ya.shape[1]), row(yb.shape[1]), pl.BlockSpec(wo.shape, lambda i: (0, 0))],
            [row(x2.shape[1])], [jax.ShapeDtypeStruct(x2.shape, F32)])


def _store_blocks(dst_ref, src, feats):
    lane = _lane_iota(src.shape[0])
    for p in range(src.shape[1] // LANES):
        pair = src[:, p * LANES:(p + 1) * LANES]
        for j in (0, 1):
            data = (lane >= HALF) if j == 1 else (lane < HALF)
            blk = jnp.where(data, pair, feats(p, j))
            dst_ref[:, (2 * p + j) * LANES:(2 * p + j + 1) * LANES] = blk.astype(BF16)


def _in_ab_kernel(*refs, tiles_per_seq, fused):
    x, refs = _residual_tile(refs, fused, 10)
    (ng_ref, w_ref, wvt_ref, gqa_ref, gka_ref, gqb_ref, gkb_ref, bf_ref, e_ref, tri_ref,
     qa_ref, ka_ref, ga_ref, qb_ref, kb_ref, gb_ref, vt_ref, km_ref, carry_ref) = refs
    i = pl.program_id(0)

    @pl.when(i % tiles_per_seq == 0)
    def _():
        carry_ref[...] = jnp.zeros_like(carry_ref)

    h = _rmsnorm_rows(x, ng_ref[...]).astype(BF16)

    def proj(c0, width):
        return jnp.dot(h, w_ref[:, c0:c0 + width], preferred_element_type=F32)

    lane, pf = _pos_features(i, tiles_per_seq)
    slopes = _alibi_slopes(8)
    rows_t = _nt(wvt_ref[...], h)
    nv = vt_ref.shape[0]
    vt_ref[...] = rows_t[:nv].astype(BF16)

    qa = _head_norm(proj(0, 512), gqa_ref[...] * SCALE, e_ref)
    _store_blocks(qa_ref, qa, lambda p, j: pf[j][0] * slopes[2 * p + j])
    ka = _head_norm(proj(512, 512), gka_ref[...], e_ref)
    _store_blocks(ka_ref, ka, lambda p, j: pf[j][1])
    km = [jnp.mean(ka[r * MOBA_BLOCK:(r + 1) * MOBA_BLOCK], axis=0, keepdims=True)
          for r in range(TM // MOBA_BLOCK)]
    km_ref[0] = jnp.concatenate(km, axis=0)
    ga_ref[...] = proj(1024, 512).astype(BF16)

    z = rows_t[nv:] + bf_ref[...]
    logf = jnp.minimum(z, 0.0) - jnp.log1p(jnp.exp(-jnp.abs(z)))
    hi = logf.astype(BF16)
    r1 = logf - hi.astype(F32)
    mid = r1.astype(BF16)
    lo = (r1 - mid.astype(F32)).astype(BF16)
    cs = jnp.dot(jnp.concatenate([hi, mid, lo], axis=0), tri_ref[...], preferred_element_type=F32)
    nh = logf.shape[0]
    ct = cs[:nh] + cs[nh:2 * nh] + cs[2 * nh:] + carry_ref[:, 0:1]
    carry_ref[...] = jnp.broadcast_to(ct[:, TM - 1:TM], carry_ref.shape)
    c = jnp.concatenate([ct, jnp.zeros((LANES - nh, TM), F32)], axis=0).T
    chi = c.astype(BF16).astype(F32)
    r1 = c - chi
    cmid = r1.astype(BF16).astype(F32)
    clo = r1 - cmid

    def bcast(arr, hh):
        return jnp.broadcast_to(arr[:, hh:hh + 1], (TM, LANES))

    def fox_q(p, j):
        hh, base = 2 * p + j, (1 - j) * HALF + F_FOX
        ones = jnp.where((lane >= base + 3) & (lane < base + 6), 1.0, 0.0)
        return jnp.where(lane == base, bcast(chi, hh),
               jnp.where(lane == base + 1, bcast(cmid, hh),
               jnp.where(lane == base + 2, bcast(clo, hh), ones)))

    def fox_k(p, j):
        hh, base = 2 * p + j, (1 - j) * HALF + F_FOX
        ones = jnp.where((lane >= base) & (lane < base + 3), 1.0, 0.0)
        return jnp.where(lane == base + 3, -bcast(chi, hh),
               jnp.where(lane == base + 4, -bcast(cmid, hh),
               jnp.where(lane == base + 5, -bcast(clo, hh), ones)))

    qb = _head_norm(proj(1536, 512), gqb_ref[...] * SCALE, e_ref)
    _store_blocks(qb_ref, qb, fox_q)
    kb = _head_norm(proj(2048, 512), gkb_ref[...], e_ref)
    _store_blocks(kb_ref, kb, fox_k)
    gb_ref[...] = proj(2560, 512).astype(BF16)


def _in_ab(x2, res, ng, w, wvt, gqa, gka, gqb, gkb, bf, e256, tri, *, seq):
    n = x2.shape[0]
    nt = n // TM
    row = lambda width: pl.BlockSpec((TM, width), lambda i: (i, 0))
    full = lambda a: pl.BlockSpec(a.shape, lambda i: (0,) * a.ndim)
    bf16 = lambda width: jax.ShapeDtypeStruct((n, width), BF16)
    res_in, res_in_specs, res_out_specs, res_out_shapes = _residual_specs(x2, res)
    consts = [ng, w, wvt, gqa, gka, gqb, gkb, bf, e256, tri]
    return pl.pallas_call(
        functools.partial(_in_ab_kernel, tiles_per_seq=seq // TM, fused=res is not None),
        grid=(nt,),
        in_specs=res_in_specs + [full(a) for a in consts],
        out_specs=res_out_specs + [row(1024), row(1024), row(512), row(1024), row(1024), row(512),
                                   pl.BlockSpec((wvt.shape[0] - BF16_ROWS, TM), lambda i: (0, i)),
                                   pl.BlockSpec((1, TM // MOBA_BLOCK, 512), lambda i: (i, 0, 0))],
        out_shape=res_out_shapes + [bf16(1024), bf16(1024), bf16(512), bf16(1024), bf16(1024), bf16(512),
                                    jax.ShapeDtypeStruct((wvt.shape[0] - BF16_ROWS, n), BF16),
                                    jax.ShapeDtypeStruct((nt, TM // MOBA_BLOCK, 512), F32)],
        scratch_shapes=[pltpu.VMEM((BF16_ROWS, LANES), F32)],
        compiler_params=_cparams(("arbitrary",)),
        name="in_proj_ab",
    )(*res_in, *consts)


def _in_cd_kernel(*refs, tiles_per_seq, fused):
    x, refs = _residual_tile(refs, fused, 8)
    (ng_ref, w_ref, wvt_ref, gqc_ref, gkc_ref, gqd_ref, gkd_ref, e_ref,
     qc_ref, kc_ref, gc_ref, qd_ref, kd_ref, gd_ref, vt_ref) = refs
    i = pl.program_id(0)
    h = _rmsnorm_rows(x, ng_ref[...]).astype(BF16)

    def proj(c0, width):
        return jnp.dot(h, w_ref[:, c0:c0 + width], preferred_element_type=F32)

    _, pf = _pos_features(i, tiles_per_seq)
    dslopes = _alibi_slopes(4)
    sslopes = _alibi_slopes(8)
    vt_ref[...] = _nt(wvt_ref[...], h).astype(BF16)

    qc = _head_norm(proj(0, 512), gqc_ref[...] * SCALE, e_ref)
    _store_blocks(qc_ref, qc, lambda p, j: pf[j][0] * dslopes[p])
    kc = _head_norm(proj(512, 512), gkc_ref[...], e_ref)
    _store_blocks(kc_ref, kc, lambda p, j: pf[j][1])
    gc_ref[...] = proj(1024, 512).astype(BF16)

    qd = _head_norm(proj(1536, 512), gqd_ref[...] * SCALE, e_ref)
    _store_blocks(qd_ref, qd, lambda p, j: pf[j][0] * sslopes[p + 4 * j])
    kd = _head_norm(proj(2048, LANES), gkd_ref[...], e_ref)
    _store_blocks(kd_ref, kd, lambda p, j: pf[j][1])
    gd_ref[...] = proj(2176, 512).astype(BF16)


def _in_cd(x2, res, ng, w, wvt, gqc, gkc, gqd, gkd, e256, *, seq):
    n = x2.shape[0]
    nt = n // TM
    row = lambda width: pl.BlockSpec((TM, width), lambda i: (i, 0))
    full = lambda a: pl.BlockSpec(a.shape, lambda i: (0,) * a.ndim)
    bf16 = lambda width: jax.ShapeDtypeStruct((n, width), BF16)
    res_in, res_in_specs, res_out_specs, res_out_shapes = _residual_specs(x2, res)
    consts = [ng, w, wvt, gqc, gkc, gqd, gkd, e256]
    return pl.pallas_call(
        functools.partial(_in_cd_kernel, tiles_per_seq=seq // TM, fused=res is not None),
        grid=(nt,),
        in_specs=res_in_specs + [full(a) for a in consts],
        out_specs=res_out_specs + [row(1024), row(1024), row(512), row(1024), row(256), row(512),
                                   pl.BlockSpec((wvt.shape[0], TM), lambda i: (0, i))],
        out_shape=res_out_shapes + [bf16(1024), bf16(1024), bf16(512), bf16(1024), bf16(256), bf16(512),
                                    jax.ShapeDtypeStruct((wvt.shape[0], n), BF16)],
        compiler_params=_cparams(("arbitrary",)),
        name="in_proj_cd",
    )(*res_in, *consts)


def _out_kernel(x_ref, ya_ref, yb_ref, w_ref, o_ref):
    half = ya_ref.shape[1]
    o_ref[...] = (x_ref[...]
                  + jnp.dot(ya_ref[...], w_ref[:half, :], preferred_element_type=F32)
                  + jnp.dot(yb_ref[...], w_ref[half:, :], preferred_element_type=F32))


def _out_proj(x2, ya, yb, w):
    n, d = x2.shape
    row = lambda width: pl.BlockSpec((TM_OUT, width), lambda i: (i, 0))
    return pl.pallas_call(
        _out_kernel,
        grid=(n // TM_OUT,),
        in_specs=[row(d), row(ya.shape[1]), row(yb.shape[1]), pl.BlockSpec(w.shape, lambda i: (0, 0))],
        out_specs=row(d),
        out_shape=jax.ShapeDtypeStruct((n, d), F32),
        compiler_params=_cparams(("arbitrary",)),
        name="out_proj",
    )(x2, ya, yb, w)


def _silu(g):
    return g * (1.0 / (1.0 + jnp.exp(-g)))


def _attend(pt, vt):
    lhs = jnp.concatenate([vt, jnp.ones((BF16_ROWS, vt.shape[1]), BF16)], axis=0)
    return jnp.dot(lhs, pt, preferred_element_type=F32)


def _causal_attend(st_main, st_last, vt, causal, sub_max):
    assert TQ == 2 * TK
    nm = st_main.shape[0]
    tri = causal[:TK, :TK]
    if sub_max:
        dark = jnp.full((TK, TQ - TK), MASK_NEG, F32)
        st = jnp.concatenate([st_main, jnp.concatenate([dark, st_last], axis=1)], axis=0)
        tail = jnp.where(causal, st[nm - TK:], MASK_NEG)
        st = tail if nm == TK else jnp.concatenate([st[:nm - TK], tail], axis=0)
        return _attend(jnp.exp(st - jnp.max(st, axis=0, keepdims=True)).astype(BF16), vt)
    own = st_main[nm - TK:]
    own = jnp.concatenate([jnp.where(tri, own[:, :TK], MASK_NEG), own[:, TK:]], axis=1)
    parts = ([st_main[:nm - TK]] if nm > TK else []) + [own]
    ot = _attend(jnp.concatenate([jnp.exp(p).astype(BF16) for p in parts], axis=0), vt[:, :nm])
    ot_last = _attend(jnp.exp(jnp.where(tri, st_last, MASK_NEG)).astype(BF16), vt[:, nm:])
    return jnp.concatenate([ot[:, :TK], ot[:, TK:] + ot_last], axis=1)


def _normalized(ot, dims):
    return ot[:dims] / ot[dims:dims + 1]


def _causal_mask_t():
    key = lax.broadcasted_iota(jnp.int32, (TQ, TQ), 0)
    qry = lax.broadcasted_iota(jnp.int32, (TQ, TQ), 1)
    return key <= qry


def _own_block(qi):
    return qi * (TQ // TK) + lax.broadcasted_iota(jnp.int32, (1, TQ), 1) // TK


def _moba_bias(gate, qi):
    first = qi * (TQ // TK)
    own = _own_block(qi)
    n_iota = lax.broadcasted_iota(jnp.int32, (8, TQ), 0)
    cnt = jnp.zeros((8, TQ), F32)
    for n2 in range(first + TQ // TK - 1):
        g2 = jnp.broadcast_to(gate[n2:n2 + 1, :], (8, TQ))
        beats = (g2 > gate) | ((g2 == gate) & (n_iota > n2))
        if n2 >= first:
            beats = beats & (own > n2)
        cnt = cnt + jnp.where(beats, 1.0, 0.0)
    return jnp.where(cnt < MOBA_TOPK, 0.0, SEL_NEG)


def _attn_ab_kernel(*refs, moba, nq, sub_max):
    if moba:
        q_ref, k_ref, vt_ref, g_ref, km_ref, o_ref = refs
    else:
        q_ref, k_ref, vt_ref, g_ref, o_ref = refs
    causal = _causal_mask_t()
    km16 = {}
    if moba:
        lane8 = lax.broadcasted_iota(jnp.int32, (8, LANES), 1)
        for u in range(PAIRS):
            for j in (0, 1):
                km = km_ref[0, :, u * LANES:(u + 1) * LANES]
                km = jnp.where((lane8 >= HALF) if j == 1 else (lane8 < HALF), km, 0.0)
                hi = km.astype(BF16)
                lo = (km - hi.astype(F32)).astype(BF16)
                km16[u, j] = jnp.concatenate([hi, lo], axis=0)
    sub = TQ // TK
    for qi in range(nq):
        r0, n = qi * TQ, (qi + 1) * TQ
        first = qi * sub
        for u in range(PAIRS):
            outs = []
            for j in (0, 1):
                c0 = (2 * u + j) * LANES
                qj = q_ref[0, r0:r0 + TQ, c0:c0 + LANES]
                kk = k_ref[0, :n - TK, c0:c0 + LANES]
                if moba and first + sub - 1 > MOBA_TOPK:
                    st = _nt(jnp.concatenate([km16[u, j], kk], axis=0), qj)
                    bias = _moba_bias(st[:8] + st[8:16], qi)
                    st = st[16:]
                    blocks = [st[nb * TK:(nb + 1) * TK] + bias[nb:nb + 1, :] for nb in range(first)]
                    blocks.append(st[first * TK:] + jnp.where(_own_block(qi) > first, bias[first:first + 1, :], 0.0))
                    st = jnp.concatenate(blocks, axis=0)
                else:
                    st = _nt(kk, qj)
                st_last = _nt(k_ref[0, n - TK:n, c0:c0 + LANES], qj[TQ - TK:])
                v0 = (2 * u + j) * HALF
                ot = _causal_attend(st, st_last, vt_ref[v0:v0 + HALF, :n], causal, sub_max)
                outs.append(_normalized(ot, HALF))
            o = jnp.concatenate(outs, axis=0).T
            gate = _silu(g_ref[0, r0:r0 + TQ, u * LANES:(u + 1) * LANES].astype(F32))
            o_ref[0, r0:r0 + TQ, u * LANES:(u + 1) * LANES] = (o * gate).astype(BF16)


def _step_spec(s, width):
    return pl.BlockSpec((1, s, PAIRS * width), lambda bi, p: (bi, 0, p))


def _vt_spec(s, row0):
    rows = PAIRS * LANES
    return pl.BlockSpec((rows, s), lambda bi, p: (row0 // rows + p, bi))


def _attn_ab(q, k, vt, g, km, *, moba, sub_max):
    b, s, _ = q.shape
    in_specs = [_step_spec(s, 2 * LANES), _step_spec(s, 2 * LANES), _vt_spec(s, 0 if moba else 4 * LANES),
                _step_spec(s, LANES)]
    args = [q, k, vt, g]
    if moba:
        in_specs.append(pl.BlockSpec((1, 8, PAIRS * LANES), lambda bi, p: (bi, 0, p)))
        args.append(km)
    return pl.pallas_call(
        functools.partial(_attn_ab_kernel, moba=moba, nq=s // TQ, sub_max=sub_max),
        grid=(b, 4 // PAIRS),
        in_specs=in_specs,
        out_specs=_step_spec(s, LANES),
        out_shape=jax.ShapeDtypeStruct((b, s, 4 * LANES), BF16),
        compiler_params=_cparams(("arbitrary", "arbitrary")),
        name=("attn_moba" if moba else "attn_fox") + ("_submax" if sub_max else ""),
    )(*args)


def _attn_diff_kernel(q_ref, k_ref, vt_ref, g_ref, lam_ref, sg_ref, o_ref, *, lam_init, nq, sub_max):
    lp = lam_ref[...]
    s1 = jnp.sum(lp[0:1] * lp[1:2], axis=-1, keepdims=True)
    s2 = jnp.sum(lp[2:3] * lp[3:4], axis=-1, keepdims=True)
    lam = jnp.exp(s1) - jnp.exp(s2) + lam_init
    causal = _causal_mask_t()
    for qi in range(nq):
        r0, n = qi * TQ, (qi + 1) * TQ
        for u in range(PAIRS):
            outs = []
            for j in (0, 1):
                c0 = (2 * u + j) * LANES
                qj = q_ref[0, r0:r0 + TQ, c0:c0 + LANES]
                st = _nt(k_ref[0, :n - TK, c0:c0 + LANES], qj)
                st_last = _nt(k_ref[0, n - TK:n, c0:c0 + LANES], qj[TQ - TK:])
                ot = _causal_attend(st, st_last, vt_ref[u * LANES:(u + 1) * LANES, :n], causal, sub_max)
                outs.append(_normalized(ot, LANES))
            oc = (outs[0] - lam * outs[1]).T
            oc = _rmsnorm_rows(oc, sg_ref[...]) * (1.0 - lam_init)
            gate = _silu(g_ref[0, r0:r0 + TQ, u * LANES:(u + 1) * LANES].astype(F32))
            o_ref[0, r0:r0 + TQ, u * LANES:(u + 1) * LANES] = (oc * gate).astype(BF16)


def _attn_diff(q, k, vt, g, lam, sg, *, lam_init, sub_max):
    b, s, _ = q.shape
    return pl.pallas_call(
        functools.partial(_attn_diff_kernel, lam_init=lam_init, nq=s // TQ, sub_max=sub_max),
        grid=(b, 4 // PAIRS),
        in_specs=[_step_spec(s, 2 * LANES), _step_spec(s, 2 * LANES), _vt_spec(s, 0),
                  _step_spec(s, LANES),
                  pl.BlockSpec(lam.shape, lambda bi, p: (0, 0)),
                  pl.BlockSpec(sg.shape, lambda bi, p: (0, 0))],
        out_specs=_step_spec(s, LANES),
        out_shape=jax.ShapeDtypeStruct((b, s, 4 * LANES), BF16),
        compiler_params=_cparams(("arbitrary", "arbitrary")),
        name="attn_diff" + ("_submax" if sub_max else ""),
    )(q, k, vt, g, lam, sg)


def _attn_swa_kernel(q_ref, k_ref, vt_ref, g_ref, sink_ref, o_ref, *, nt, sub_max):
    w = SWA_BLOCK
    key = lax.broadcasted_iota(jnp.int32, (w, w), 0)
    qry = lax.broadcasted_iota(jnp.int32, (w, w), 1)
    bias = {0: jnp.where(key <= qry, 0.0, MASK_NEG),
            1: jnp.where(key > qry, 0.0, MASK_NEG)}
    dark = jnp.full((w, w), MASK_NEG, F32)
    for ti in range(nt):
        t0 = ti * TS
        start = max(t0 - w, 0)
        nkeys = t0 + TS - start
        heads = []
        for j in (0, 1):
            qs = jnp.concatenate([q_ref[0, t0:t0 + TS, (2 * p + j) * LANES:(2 * p + j + 1) * LANES]
                                  for p in range(4)], axis=0)
            st = _nt(k_ref[0, start:start + nkeys, j * LANES:(j + 1) * LANES], qs)
            sink = sink_ref[j:j + 1, :]

            def blocks(fn):
                rows = []
                for r in range(nkeys // w):
                    cols = []
                    for c in range(4 * TS // w):
                        dist = (t0 // w + c % (TS // w)) - (start // w + r)
                        cols.append(fn(st[r * w:(r + 1) * w, c * w:(c + 1) * w], dist))
                    rows.append(jnp.concatenate(cols, axis=1))
                return jnp.concatenate(rows, axis=0)

            if sub_max:
                st = blocks(lambda blk, d: blk + bias.get(d, dark))
                m = jnp.maximum(jnp.max(st, axis=0, keepdims=True), sink)
                pt, sink = jnp.exp(st - m).astype(BF16), sink - m
            else:
                pt = blocks(lambda blk, d: jnp.exp(blk + bias[d]).astype(BF16) if d in bias
                            else jnp.zeros((w, w), BF16))
            ot = _attend(pt, vt_ref[j * HALF:(j + 1) * HALF, start:start + nkeys])
            heads.append(ot[:HALF] / (ot[HALF:HALF + 1] + jnp.exp(sink)))
        for p in range(4):
            o = jnp.concatenate([hd[:, p * TS:(p + 1) * TS] for hd in heads], axis=0).T
            gate = _silu(g_ref[0, t0:t0 + TS, p * LANES:(p + 1) * LANES].astype(F32))
            o_ref[0, t0:t0 + TS, p * LANES:(p + 1) * LANES] = (o * gate).astype(BF16)


def _attn_swa(q, k, vt, g, sinks, *, sub_max):
    b, s, _ = q.shape
    whole = lambda width: pl.BlockSpec((1, s, width), lambda bi: (bi, 0, 0))
    return pl.pallas_call(
        functools.partial(_attn_swa_kernel, nt=s // TS, sub_max=sub_max),
        grid=(b,),
        in_specs=[whole(8 * LANES), whole(2 * LANES), pl.BlockSpec((LANES, s), lambda bi: (4, bi)),
                  whole(4 * LANES), pl.BlockSpec(sinks.shape, lambda bi: (0, 0))],
        out_specs=whole(4 * LANES),
        out_shape=jax.ShapeDtypeStruct((b, s, 4 * LANES), BF16),
        compiler_params=_cparams(("arbitrary",)),
        name="attn_swa" + ("_submax" if sub_max else ""),
    )(q, k, vt, g, sinks)


SWA_HEAD_ORDER = (0, 4, 1, 5, 2, 6, 3, 7)


def _tile_gain(g, reps):
    return jnp.tile(g.astype(F32), reps)[None, :]


def _logit_bound(gq, gk):
    return HEAD_DIM * SCALE * jnp.max(jnp.abs(gq)) * jnp.max(jnp.abs(gk))


def _guarded(bound, fn, *args):
    return lax.cond(bound <= LOGIT_SAFE, functools.partial(fn, sub_max=False),
                    functools.partial(fn, sub_max=True), *args)


def kernel(x, norm_gain, w_in_ab, b_forget, moba_q_gain, moba_k_gain, fox_q_gain, fox_k_gain, w_out_ab,
           w_in_cd, diff_q_gain, diff_k_gain, diff_lambda, diff_subln_gain, swa_q_gain, swa_k_gain,
           swa_sinks, w_out_cd):
    b, s, d = x.shape
    assert d == 1024 and s % TM == 0 and s // MOBA_BLOCK == 8
    depth = norm_gain.shape[0]
    x2 = x.reshape(b * s, d)

    grp = np.arange(256) // HEAD_DIM
    e256 = jnp.asarray(grp[:, None] == grp[None, :], dtype=BF16)
    tri = jnp.asarray(np.triu(np.ones((TM, TM), np.float32)), dtype=BF16)
    perm = np.concatenate([np.arange(HEAD_DIM) + HEAD_DIM * hh for hh in SWA_HEAD_ORDER])
    r3 = lambda a: a.reshape(b, s, a.shape[1])

    res = None
    for layer in range(depth):
        j = layer // 2
        ng = norm_gain[layer][None, :]
        if layer % 2 == 0:
            wj = w_in_ab[j]
            w = jnp.concatenate([wj[:, 0:1024], wj[:, 1536:2048], wj[:, 2048:3072], wj[:, 3584:4096]],
                                axis=1).astype(BF16)
            nf = wj.shape[1] - 4096
            wvt = jnp.concatenate([wj[:, 1024:1536], wj[:, 3072:3584],
                                   jnp.pad(wj[:, 4096:], ((0, 0), (0, BF16_ROWS - nf)))], axis=1).T.astype(BF16)
            bf = jnp.pad(b_forget[j].astype(F32), (0, BF16_ROWS - nf))[:, None]
            outs = _in_ab(
                x2, res, ng, w, wvt, _tile_gain(moba_q_gain[j], 8), _tile_gain(moba_k_gain[j], 8),
                _tile_gain(fox_q_gain[j], 8), _tile_gain(fox_k_gain[j], 8), bf, e256, tri, seq=s)
            if res is not None:
                x2, outs = outs[0], outs[1:]
            qa, ka, ga, qb, kb, gb, vt, km = outs
            ya = _guarded(_logit_bound(moba_q_gain[j], moba_k_gain[j]), functools.partial(_attn_ab, moba=True),
                          r3(qa), r3(ka), vt, r3(ga), km.reshape(b, 8, 512))
            yb = _guarded(_logit_bound(fox_q_gain[j], fox_k_gain[j]), functools.partial(_attn_ab, moba=False),
                          r3(qb), r3(kb), vt, r3(gb), None)
            res = (ya.reshape(b * s, 512), yb.reshape(b * s, 512), w_out_ab[j].astype(BF16))
        else:
            wj = w_in_cd[j]
            w = jnp.concatenate([wj[:, 0:1024], wj[:, 1536:2048], wj[:, 2048:2560][:, perm],
                                 wj[:, 2560:2688], wj[:, 2816:3328][:, perm]], axis=1).astype(BF16)
            wvt = jnp.concatenate([wj[:, 1024:1536], wj[:, 2688:2816]], axis=1).T.astype(BF16)
            outs = _in_cd(
                x2, res, ng, w, wvt, _tile_gain(diff_q_gain[j], 8), _tile_gain(diff_k_gain[j], 8),
                _tile_gain(swa_q_gain[j], 8), _tile_gain(swa_k_gain[j], 2), e256, seq=s)
            if res is not None:
                x2, outs = outs[0], outs[1:]
            qc, kc, gc, qd, kd, gd, vt = outs
            lam_init = 0.8 - 0.6 * math.exp(-0.3 * layer)
            yc = _guarded(_logit_bound(diff_q_gain[j], diff_k_gain[j]),
                          functools.partial(_attn_diff, lam_init=lam_init),
                          r3(qc), r3(kc), vt, r3(gc), diff_lambda[j].astype(F32),
                          diff_subln_gain[j].astype(F32)[None, :])
            sinks = swa_sinks[j].astype(F32).reshape(2, 4)
            swa_bound = jnp.maximum(_logit_bound(swa_q_gain[j], swa_k_gain[j]), jnp.max(sinks))
            yd = _guarded(swa_bound, _attn_swa,
                          r3(qd), r3(kd), vt, r3(gd), jnp.repeat(sinks, TS, axis=1))
            wo = w_out_cd[j]
            wo = jnp.concatenate([wo[:512], wo[512:][perm]], axis=0).astype(BF16)
            res = (yc.reshape(b * s, 512), yd.reshape(b * s, 512), wo)
    return _out_proj(x2, *res).reshape(b, s, d)
```

```python
import functools
import math

import numpy as np
import jax
import jax.numpy as jnp
from jax import lax
from jax.experimental import pallas as pl
from jax.experimental.pallas import tpu as pltpu

F32 = jnp.float32
BF16 = jnp.bfloat16

LANES = 128
BF16_ROWS = 16
HEAD_DIM = 64
HALF = 64
EPS = 1e-6
SCALE = HEAD_DIM ** -0.5
TM = 512
TM_OUT = 1024
TQ = 512
TK = 256
PAIRS = 2
TS = 256
SWA_BLOCK = 128
MOBA_BLOCK = 256
MOBA_TOPK = 3
MASK_NEG = -1e30
SEL_NEG = -1e9
LOGIT_SAFE = 32.0
VMEM_LIMIT = 56 * 1024 * 1024

F_ALIBI = 0
F_FOX = 0


def _alibi_slopes(n):
    s = [2.0 ** (-8.0 * (i + 1) / n) for i in range(n)]
    for v in s:
        m, _ = math.frexp(v)
        assert m == 0.5, "feature-lane ALiBi needs power-of-two slopes"
    return s


def _cparams(sem):
    return pltpu.CompilerParams(dimension_semantics=sem, vmem_limit_bytes=VMEM_LIMIT)


def _nt(a, b):
    return lax.dot_general(a, b, (((1,), (1,)), ((), ())), preferred_element_type=F32)


def _group_sums(sq, e_ref):
    width = sq.shape[1]
    if width == LANES:
        return jnp.dot(sq.astype(BF16), e_ref[:LANES, :LANES], preferred_element_type=F32)
    e = e_ref[...]
    outs = [jnp.dot(sq[:, c:c + 256].astype(BF16), e, preferred_element_type=F32)
            for c in range(0, width, 256)]
    return outs[0] if len(outs) == 1 else jnp.concatenate(outs, axis=1)


def _head_norm(p, gain_row, e_ref):
    ssq = _group_sums(p * p, e_ref)
    return p * lax.rsqrt(ssq * (1.0 / HEAD_DIM) + EPS) * gain_row


def _lane_iota(rows):
    return lax.broadcasted_iota(jnp.int32, (rows, LANES), 1)


def _pos_features(i, tiles_per_seq):
    lane = _lane_iota(TM)
    pos = (i % tiles_per_seq) * TM + lax.broadcasted_iota(jnp.int32, (TM, LANES), 0)
    a64 = ((pos >> 6) << 6).astype(F32)
    a = (pos >> 6).astype(F32)
    b = (pos & 63).astype(F32)
    out = []
    for j in (0, 1):
        base = (1 - j) * HALF + F_ALIBI
        qb = jnp.where(lane == base, -a64,
             jnp.where(lane == base + 1, -b,
             jnp.where(lane == base + 2, 64.0,
             jnp.where(lane == base + 3, 1.0, 0.0))))
        kf = jnp.where((lane == base) | (lane == base + 1), 1.0,
             jnp.where(lane == base + 2, a,
             jnp.where(lane == base + 3, b, 0.0)))
        out.append((qb, kf))
    return lane, out


def _rmsnorm_rows(x, gain_row):
    return x * lax.rsqrt(jnp.mean(x * x, axis=-1, keepdims=True) + EPS) * gain_row


def _residual_tile(refs, fused, n_in):
    if not fused:
        return refs[0][...], refs[1:]
    x_ref, ya_ref, yb_ref, wo_ref = refs[:4]
    rest = refs[4:]
    half = ya_ref.shape[1]
    x = (x_ref[...]
         + jnp.dot(ya_ref[...], wo_ref[:half, :], preferred_element_type=F32)
         + jnp.dot(yb_ref[...], wo_ref[half:, :], preferred_element_type=F32))
    rest[n_in][...] = x
    return x, rest[:n_in] + rest[n_in + 1:]


def _residual_specs(x2, res):
    row = lambda width: pl.BlockSpec((TM, width), lambda i: (i, 0))
    if res is None:
        return [x2], [row(x2.shape[1])], [], []
    ya, yb, wo = res
    return ([x2, ya, yb, wo],
            [row(x2.shape[1]), row(ya.shape[1]), row(yb.shape[1]), pl.BlockSpec(wo.shape, lambda i: (0, 0))],
            [row(x2.shape[1])], [jax.ShapeDtypeStruct(x2.shape, F32)])


def _store_blocks(dst_ref, src, feats, pair0=0):
    lane = _lane_iota(src.shape[0])
    for q in range(src.shape[1] // LANES):
        p = pair0 + q
        pair = src[:, q * LANES:(q + 1) * LANES]
        for j in (0, 1):
            data = (lane >= HALF) if j == 1 else (lane < HALF)
            blk = jnp.where(data, pair, feats(p, j))
            dst_ref[:, (2 * p + j) * LANES:(2 * p + j + 1) * LANES] = blk.astype(BF16)


def _norm_store(dst_ref, p, gain_row, e_ref, feats):
    width = min(2 * LANES, p.shape[1])
    halves = []
    for c in range(0, p.shape[1], width):
        qn = _head_norm(p[:, c:c + width], gain_row[:, c:c + width], e_ref)
        _store_blocks(dst_ref, qn, feats, pair0=c // LANES)
        halves.append(qn)
    return halves


def _in_ab_kernel(*refs, tiles_per_seq, fused):
    x, refs = _residual_tile(refs, fused, 10)
    (ng_ref, w_ref, wvt_ref, gqa_ref, gka_ref, gqb_ref, gkb_ref, bf_ref, e_ref, tri_ref,
     qa_ref, ka_ref, ga_ref, qb_ref, kb_ref, gb_ref, vt_ref, km_ref, carry_ref) = refs
    i = pl.program_id(0)

    @pl.when(i % tiles_per_seq == 0)
    def _():
        carry_ref[...] = jnp.zeros_like(carry_ref)

    h = _rmsnorm_rows(x, ng_ref[...]).astype(BF16)

    def proj(c0, width):
        return jnp.dot(h, w_ref[:, c0:c0 + width], preferred_element_type=F32)

    lane, pf = _pos_features(i, tiles_per_seq)
    slopes = _alibi_slopes(8)
    rows_t = _nt(wvt_ref[...], h)
    nv = vt_ref.shape[0]
    vt_ref[...] = rows_t[:nv].astype(BF16)

    _norm_store(qa_ref, proj(0, 512), gqa_ref[...] * SCALE, e_ref, lambda p, j: pf[j][0] * slopes[2 * p + j])
    ka = _norm_store(ka_ref, proj(512, 512), gka_ref[...], e_ref, lambda p, j: pf[j][1])
    km = [jnp.concatenate([jnp.mean(half[r * MOBA_BLOCK:(r + 1) * MOBA_BLOCK], axis=0, keepdims=True)
                           for half in ka], axis=1) for r in range(TM // MOBA_BLOCK)]
    km_ref[0] = jnp.concatenate(km, axis=0)
    ga_ref[...] = proj(1024, 512).astype(BF16)

    z = rows_t[nv:] + bf_ref[...]
    logf = jnp.minimum(z, 0.0) - jnp.log1p(jnp.exp(-jnp.abs(z)))
    hi = logf.astype(BF16)
    r1 = logf - hi.astype(F32)
    mid = r1.astype(BF16)
    lo = (r1 - mid.astype(F32)).astype(BF16)
    cs = jnp.dot(jnp.concatenate([hi, mid, lo], axis=0), tri_ref[...], preferred_element_type=F32)
    nh = logf.shape[0]
    ct = cs[:nh] + cs[nh:2 * nh] + cs[2 * nh:] + carry_ref[:, 0:1]
    carry_ref[...] = jnp.broadcast_to(ct[:, TM - 1:TM], carry_ref.shape)
    c = jnp.concatenate([ct, jnp.zeros((LANES - nh, TM), F32)], axis=0).T
    chi = c.astype(BF16).astype(F32)
    r1 = c - chi
    cmid = r1.astype(BF16).astype(F32)
    clo = r1 - cmid

    def bcast(arr, hh):
        return jnp.broadcast_to(arr[:, hh:hh + 1], (TM, LANES))

    def fox_q(p, j):
        hh, base = 2 * p + j, (1 - j) * HALF + F_FOX
        ones = jnp.where((lane >= base + 3) & (lane < base + 6), 1.0, 0.0)
        return jnp.where(lane == base, bcast(chi, hh),
               jnp.where(lane == base + 1, bcast(cmid, hh),
               jnp.where(lane == base + 2, bcast(clo, hh), ones)))

    def fox_k(p, j):
        hh, base = 2 * p + j, (1 - j) * HALF + F_FOX
        ones = jnp.where((lane >= base) & (lane < base + 3), 1.0, 0.0)
        return jnp.where(lane == base + 3, -bcast(chi, hh),
               jnp.where(lane == base + 4, -bcast(cmid, hh),
               jnp.where(lane == base + 5, -bcast(clo, hh), ones)))

    _norm_store(qb_ref, proj(1536, 512), gqb_ref[...] * SCALE, e_ref, fox_q)
    _norm_store(kb_ref, proj(2048, 512), gkb_ref[...], e_ref, fox_k)
    gb_ref[...] = proj(2560, 512).astype(BF16)


def _in_ab(x2, res, ng, w, wvt, gqa, gka, gqb, gkb, bf, e256, tri, *, seq):
    n = x2.shape[0]
    nt = n // TM
    row = lambda width: pl.BlockSpec((TM, width), lambda i: (i, 0))
    full = lambda a: pl.BlockSpec(a.shape, lambda i: (0,) * a.ndim)
    bf16 = lambda width: jax.ShapeDtypeStruct((n, width), BF16)
    res_in, res_in_specs, res_out_specs, res_out_shapes = _residual_specs(x2, res)
    consts = [ng, w, wvt, gqa, gka, gqb, gkb, bf, e256, tri]
    return pl.pallas_call(
        functools.partial(_in_ab_kernel, tiles_per_seq=seq // TM, fused=res is not None),
        grid=(nt,),
        in_specs=res_in_specs + [full(a) for a in consts],
        out_specs=res_out_specs + [row(1024), row(1024), row(512), row(1024), row(1024), row(512),
                                   pl.BlockSpec((wvt.shape[0] - BF16_ROWS, TM), lambda i: (0, i)),
                                   pl.BlockSpec((1, TM // MOBA_BLOCK, 512), lambda i: (i, 0, 0))],
        out_shape=res_out_shapes + [bf16(1024), bf16(1024), bf16(512), bf16(1024), bf16(1024), bf16(512),
                                    jax.ShapeDtypeStruct((wvt.shape[0] - BF16_ROWS, n), BF16),
                                    jax.ShapeDtypeStruct((nt, TM // MOBA_BLOCK, 512), F32)],
        scratch_shapes=[pltpu.VMEM((BF16_ROWS, LANES), F32)],
        compiler_params=_cparams(("arbitrary",)),
        name="in_proj_ab",
    )(*res_in, *consts)


def _in_cd_kernel(*refs, tiles_per_seq, fused):
    x, refs = _residual_tile(refs, fused, 8)
    (ng_ref, w_ref, wvt_ref, gqc_ref, gkc_ref, gqd_ref, gkd_ref, e_ref,
     qc_ref, kc_ref, gc_ref, qd_ref, kd_ref, gd_ref, vt_ref) = refs
    i = pl.program_id(0)
    h = _rmsnorm_rows(x, ng_ref[...]).astype(BF16)

    def proj(c0, width):
        return jnp.dot(h, w_ref[:, c0:c0 + width], preferred_element_type=F32)

    _, pf = _pos_features(i, tiles_per_seq)
    dslopes = _alibi_slopes(4)
    sslopes = _alibi_slopes(8)
    vt_ref[...] = _nt(wvt_ref[...], h).astype(BF16)

    _norm_store(qc_ref, proj(0, 512), gqc_ref[...] * SCALE, e_ref, lambda p, j: pf[j][0] * dslopes[p])
    _norm_store(kc_ref, proj(512, 512), gkc_ref[...], e_ref, lambda p, j: pf[j][1])
    gc_ref[...] = proj(1024, 512).astype(BF16)

    _norm_store(qd_ref, proj(1536, 512), gqd_ref[...] * SCALE, e_ref, lambda p, j: pf[j][0] * sslopes[p + 4 * j])
    _norm_store(kd_ref, proj(2048, LANES), gkd_ref[...], e_ref, lambda p, j: pf[j][1])
    gd_ref[...] = proj(2176, 512).astype(BF16)


def _in_cd(x2, res, ng, w, wvt, gqc, gkc, gqd, gkd, e256, *, seq):
    n = x2.shape[0]
    nt = n // TM
    row = lambda width: pl.BlockSpec((TM, width), lambda i: (i, 0))
    full = lambda a: pl.BlockSpec(a.shape, lambda i: (0,) * a.ndim)
    bf16 = lambda width: jax.ShapeDtypeStruct((n, width), BF16)
    res_in, res_in_specs, res_out_specs, res_out_shapes = _residual_specs(x2, res)
    consts = [ng, w, wvt, gqc, gkc, gqd, gkd, e256]
    return pl.pallas_call(
        functools.partial(_in_cd_kernel, tiles_per_seq=seq // TM, fused=res is not None),
        grid=(nt,),
        in_specs=res_in_specs + [full(a) for a in consts],
        out_specs=res_out_specs + [row(1024), row(1024), row(512), row(1024), row(256), row(512),
                                   pl.BlockSpec((wvt.shape[0], TM), lambda i: (0, i))],
        out_shape=res_out_shapes + [bf16(1024), bf16(1024), bf16(512), bf16(1024), bf16(256), bf16(512),
                                    jax.ShapeDtypeStruct((wvt.shape[0], n), BF16)],
        compiler_params=_cparams(("arbitrary",)),
        name="in_proj_cd",
    )(*res_in, *consts)


def _out_kernel(x_ref, ya_ref, yb_ref, w_ref, o_ref):
    half = ya_ref.shape[1]
    o_ref[...] = (x_ref[...]
                  + jnp.dot(ya_ref[...], w_ref[:half, :], preferred_element_type=F32)
                  + jnp.dot(yb_ref[...], w_ref[half:, :], preferred_element_type=F32))


def _out_proj(x2, ya, yb, w):
    n, d = x2.shape
    row = lambda width: pl.BlockSpec((TM_OUT, width), lambda i: (i, 0))
    return pl.pallas_call(
        _out_kernel,
        grid=(n // TM_OUT,),
        in_specs=[row(d), row(ya.shape[1]), row(yb.shape[1]), pl.BlockSpec(w.shape, lambda i: (0, 0))],
        out_specs=row(d),
        out_shape=jax.ShapeDtypeStruct((n, d), F32),
        compiler_params=_cparams(("arbitrary",)),
        name="out_proj",
    )(x2, ya, yb, w)


def _silu(g):
    return g * (1.0 / (1.0 + jnp.exp(-g)))


def _attend(pt, vt):
    lhs = jnp.concatenate([vt, jnp.ones((BF16_ROWS, vt.shape[1]), BF16)], axis=0)
    return jnp.dot(lhs, pt, preferred_element_type=F32)


def _causal_attend(st_main, st_last, vt, causal, sub_max):
    assert TQ == 2 * TK
    nm = st_main.shape[0]
    tri = causal[:TK, :TK]
    if sub_max:
        dark = jnp.full((TK, TQ - TK), MASK_NEG, F32)
        st = jnp.concatenate([st_main, jnp.concatenate([dark, st_last], axis=1)], axis=0)
        tail = jnp.where(causal, st[nm - TK:], MASK_NEG)
        st = tail if nm == TK else jnp.concatenate([st[:nm - TK], tail], axis=0)
        return _attend(jnp.exp(st - jnp.max(st, axis=0, keepdims=True)).astype(BF16), vt)
    own = st_main[nm - TK:]
    own = jnp.concatenate([jnp.where(tri, own[:, :TK], MASK_NEG), own[:, TK:]], axis=1)
    parts = ([st_main[:nm - TK]] if nm > TK else []) + [own]
    ot = _attend(jnp.concatenate([jnp.exp(p).astype(BF16) for p in parts], axis=0), vt[:, :nm])
    ot_last = _attend(jnp.exp(jnp.where(tri, st_last, MASK_NEG)).astype(BF16), vt[:, nm:])
    return jnp.concatenate([ot[:, :TK], ot[:, TK:] + ot_last], axis=1)


def _normalized(ot, dims):
    return ot[:dims] / ot[dims:dims + 1]


def _causal_mask_t():
    key = lax.broadcasted_iota(jnp.int32, (TQ, TQ), 0)
    qry = lax.broadcasted_iota(jnp.int32, (TQ, TQ), 1)
    return key <= qry


def _own_block(qi):
    return qi * (TQ // TK) + lax.broadcasted_iota(jnp.int32, (1, TQ), 1) // TK


def _moba_bias(gate, qi):
    first = qi * (TQ // TK)
    own = _own_block(qi)
    n_iota = lax.broadcasted_iota(jnp.int32, (8, TQ), 0)
    cnt = jnp.zeros((8, TQ), F32)
    for n2 in range(first + TQ // TK - 1):
        g2 = jnp.broadcast_to(gate[n2:n2 + 1, :], (8, TQ))
        beats = (g2 > gate) | ((g2 == gate) & (n_iota > n2))
        if n2 >= first:
            beats = beats & (own > n2)
        cnt = cnt + jnp.where(beats, 1.0, 0.0)
    return jnp.where(cnt < MOBA_TOPK, 0.0, SEL_NEG)


def _attn_ab_kernel(*refs, moba, nq, sub_max):
    if moba:
        q_ref, k_ref, vt_ref, g_ref, km_ref, o_ref = refs
    else:
        q_ref, k_ref, vt_ref, g_ref, o_ref = refs
    causal = _causal_mask_t()
    km16 = {}
    if moba:
        lane8 = lax.broadcasted_iota(jnp.int32, (8, LANES), 1)
        for u in range(PAIRS):
            for j in (0, 1):
                km = km_ref[0, :, u * LANES:(u + 1) * LANES]
                km = jnp.where((lane8 >= HALF) if j == 1 else (lane8 < HALF), km, 0.0)
                hi = km.astype(BF16)
                lo = (km - hi.astype(F32)).astype(BF16)
                km16[u, j] = jnp.concatenate([hi, lo], axis=0)
    sub = TQ // TK
    for qi in range(nq):
        r0, n = qi * TQ, (qi + 1) * TQ
        first = qi * sub
        for u in range(PAIRS):
            outs = []
            for j in (0, 1):
                c0 = (2 * u + j) * LANES
                qj = q_ref[0, r0:r0 + TQ, c0:c0 + LANES]
                kk = k_ref[0, :n - TK, c0:c0 + LANES]
                if moba and first + sub - 1 > MOBA_TOPK:
                    st = _nt(jnp.concatenate([km16[u, j], kk], axis=0), qj)
                    bias = _moba_bias(st[:8] + st[8:16], qi)
                    st = st[16:]
                    blocks = [st[nb * TK:(nb + 1) * TK] + bias[nb:nb + 1, :] for nb in range(first)]
                    blocks.append(st[first * TK:] + jnp.where(_own_block(qi) > first, bias[first:first + 1, :], 0.0))
                    st = jnp.concatenate(blocks, axis=0)
                else:
                    st = _nt(kk, qj)
                st_last = _nt(k_ref[0, n - TK:n, c0:c0 + LANES], qj[TQ - TK:])
                v0 = (2 * u + j) * HALF
                ot = _causal_attend(st, st_last, vt_ref[v0:v0 + HALF, :n], causal, sub_max)
                outs.append(_normalized(ot, HALF))
            o = jnp.concatenate(outs, axis=0).T
            gate = _silu(g_ref[0, r0:r0 + TQ, u * LANES:(u + 1) * LANES].astype(F32))
            o_ref[0, r0:r0 + TQ, u * LANES:(u + 1) * LANES] = (o * gate).astype(BF16)


def _step_spec(s, width):
    return pl.BlockSpec((1, s, PAIRS * width), lambda bi, p: (bi, 0, p))


def _vt_spec(s, row0):
    rows = PAIRS * LANES
    return pl.BlockSpec((rows, s), lambda bi, p: (row0 // rows + p, bi))


def _attn_ab(q, k, vt, g, km, *, moba, sub_max):
    b, s, _ = q.shape
    in_specs = [_step_spec(s, 2 * LANES), _step_spec(s, 2 * LANES), _vt_spec(s, 0 if moba else 4 * LANES),
                _step_spec(s, LANES)]
    args = [q, k, vt, g]
    if moba:
        in_specs.append(pl.BlockSpec((1, 8, PAIRS * LANES), lambda bi, p: (bi, 0, p)))
        args.append(km)
    return pl.pallas_call(
        functools.partial(_attn_ab_kernel, moba=moba, nq=s // TQ, sub_max=sub_max),
        grid=(b, 4 // PAIRS),
        in_specs=in_specs,
        out_specs=_step_spec(s, LANES),
        out_shape=jax.ShapeDtypeStruct((b, s, 4 * LANES), BF16),
        compiler_params=_cparams(("arbitrary", "arbitrary")),
        name=("attn_moba" if moba else "attn_fox") + ("_submax" if sub_max else ""),
    )(*args)


def _attn_diff_kernel(q_ref, k_ref, vt_ref, g_ref, lam_ref, sg_ref, o_ref, *, lam_init, nq, sub_max):
    lp = lam_ref[...]
    s1 = jnp.sum(lp[0:1] * lp[1:2], axis=-1, keepdims=True)
    s2 = jnp.sum(lp[2:3] * lp[3:4], axis=-1, keepdims=True)
    lam = jnp.exp(s1) - jnp.exp(s2) + lam_init
    causal = _causal_mask_t()
    for qi in range(nq):
        r0, n = qi * TQ, (qi + 1) * TQ
        for u in range(PAIRS):
            outs = []
            for j in (0, 1):
                c0 = (2 * u + j) * LANES
                qj = q_ref[0, r0:r0 + TQ, c0:c0 + LANES]
                st = _nt(k_ref[0, :n - TK, c0:c0 + LANES], qj)
                st_last = _nt(k_ref[0, n - TK:n, c0:c0 + LANES], qj[TQ - TK:])
                ot = _causal_attend(st, st_last, vt_ref[u * LANES:(u + 1) * LANES, :n], causal, sub_max)
                outs.append(_normalized(ot, LANES))
            oc = (outs[0] - lam * outs[1]).T
            oc = _rmsnorm_rows(oc, sg_ref[...]) * (1.0 - lam_init)
            gate = _silu(g_ref[0, r0:r0 + TQ, u * LANES:(u + 1) * LANES].astype(F32))
            o_ref[0, r0:r0 + TQ, u * LANES:(u + 1) * LANES] = (oc * gate).astype(BF16)


def _attn_diff(q, k, vt, g, lam, sg, *, lam_init, sub_max):
    b, s, _ = q.shape
    return pl.pallas_call(
        functools.partial(_attn_diff_kernel, lam_init=lam_init, nq=s // TQ, sub_max=sub_max),
        grid=(b, 4 // PAIRS),
        in_specs=[_step_spec(s, 2 * LANES), _step_spec(s, 2 * LANES), _vt_spec(s, 0),
                  _step_spec(s, LANES),
                  pl.BlockSpec(lam.shape, lambda bi, p: (0, 0)),
                  pl.BlockSpec(sg.shape, lambda bi, p: (0, 0))],
        out_specs=_step_spec(s, LANES),
        out_shape=jax.ShapeDtypeStruct((b, s, 4 * LANES), BF16),
        compiler_params=_cparams(("arbitrary", "arbitrary")),
        name="attn_diff" + ("_submax" if sub_max else ""),
    )(q, k, vt, g, lam, sg)


def _attn_swa_kernel(q_ref, k_ref, vt_ref, g_ref, sink_ref, o_ref, *, nt, sub_max):
    w = SWA_BLOCK
    key = lax.broadcasted_iota(jnp.int32, (w, w), 0)
    qry = lax.broadcasted_iota(jnp.int32, (w, w), 1)
    bias = {0: jnp.where(key <= qry, 0.0, MASK_NEG),
            1: jnp.where(key > qry, 0.0, MASK_NEG)}
    dark = jnp.full((w, w), MASK_NEG, F32)
    for ti in range(nt):
        t0 = ti * TS
        start = max(t0 - w, 0)
        nkeys = t0 + TS - start
        heads = []
        for j in (0, 1):
            qs = jnp.concatenate([q_ref[0, t0:t0 + TS, (2 * p + j) * LANES:(2 * p + j + 1) * LANES]
                                  for p in range(4)], axis=0)
            st = _nt(k_ref[0, start:start + nkeys, j * LANES:(j + 1) * LANES], qs)
            sink = sink_ref[j:j + 1, :]

            def blocks(fn):
                rows = []
                for r in range(nkeys // w):
                    cols = []
                    for c in range(4 * TS // w):
                        dist = (t0 // w + c % (TS // w)) - (start // w + r)
                        cols.append(fn(st[r * w:(r + 1) * w, c * w:(c + 1) * w], dist))
                    rows.append(jnp.concatenate(cols, axis=1))
                return jnp.concatenate(rows, axis=0)

            if sub_max:
                st = blocks(lambda blk, d: blk + bias.get(d, dark))
                m = jnp.maximum(jnp.max(st, axis=0, keepdims=True), sink)
                pt, sink = jnp.exp(st - m).astype(BF16), sink - m
            else:
                pt = blocks(lambda blk, d: jnp.exp(blk + bias[d]).astype(BF16) if d in bias
                            else jnp.zeros((w, w), BF16))
            ot = _attend(pt, vt_ref[j * HALF:(j + 1) * HALF, start:start + nkeys])
            heads.append(ot[:HALF] / (ot[HALF:HALF + 1] + jnp.exp(sink)))
        for p in range(4):
            o = jnp.concatenate([hd[:, p * TS:(p + 1) * TS] for hd in heads], axis=0).T
            gate = _silu(g_ref[0, t0:t0 + TS, p * LANES:(p + 1) * LANES].astype(F32))
            o_ref[0, t0:t0 + TS, p * LANES:(p + 1) * LANES] = (o * gate).astype(BF16)


def _attn_swa(q, k, vt, g, sinks, *, sub_max):
    b, s, _ = q.shape
    whole = lambda width: pl.BlockSpec((1, s, width), lambda bi: (bi, 0, 0))
    return pl.pallas_call(
        functools.partial(_attn_swa_kernel, nt=s // TS, sub_max=sub_max),
        grid=(b,),
        in_specs=[whole(8 * LANES), whole(2 * LANES), pl.BlockSpec((LANES, s), lambda bi: (4, bi)),
                  whole(4 * LANES), pl.BlockSpec(sinks.shape, lambda bi: (0, 0))],
        out_specs=whole(4 * LANES),
        out_shape=jax.ShapeDtypeStruct((b, s, 4 * LANES), BF16),
        compiler_params=_cparams(("arbitrary",)),
        name="attn_swa" + ("_submax" if sub_max else ""),
    )(q, k, vt, g, sinks)


SWA_HEAD_ORDER = (0, 4, 1, 5, 2, 6, 3, 7)


def _tile_gain(g, reps):
    return jnp.tile(g.astype(F32), reps)[None, :]


def _logit_bound(gq, gk):
    return HEAD_DIM * SCALE * jnp.max(jnp.abs(gq)) * jnp.max(jnp.abs(gk))


def _guarded(bound, fn, *args):
    return lax.cond(bound <= LOGIT_SAFE, functools.partial(fn, sub_max=False),
                    functools.partial(fn, sub_max=True), *args)


def kernel(x, norm_gain, w_in_ab, b_forget, moba_q_gain, moba_k_gain, fox_q_gain, fox_k_gain, w_out_ab,
           w_in_cd, diff_q_gain, diff_k_gain, diff_lambda, diff_subln_gain, swa_q_gain, swa_k_gain,
           swa_sinks, w_out_cd):
    b, s, d = x.shape
    assert d == 1024 and s % TM == 0 and s // MOBA_BLOCK == 8
    depth = norm_gain.shape[0]
    x2 = x.reshape(b * s, d)

    grp = np.arange(256) // HEAD_DIM
    e256 = jnp.asarray(grp[:, None] == grp[None, :], dtype=BF16)
    tri = jnp.asarray(np.triu(np.ones((TM, TM), np.float32)), dtype=BF16)
    perm = np.concatenate([np.arange(HEAD_DIM) + HEAD_DIM * hh for hh in SWA_HEAD_ORDER])
    r3 = lambda a: a.reshape(b, s, a.shape[1])

    res = None
    for layer in range(depth):
        j = layer // 2
        ng = norm_gain[layer][None, :]
        if layer % 2 == 0:
            wj = w_in_ab[j]
            w = jnp.concatenate([wj[:, 0:1024], wj[:, 1536:2048], wj[:, 2048:3072], wj[:, 3584:4096]],
                                axis=1).astype(BF16)
            nf = wj.shape[1] - 4096
            wvt = jnp.concatenate([wj[:, 1024:1536], wj[:, 3072:3584],
                                   jnp.pad(wj[:, 4096:], ((0, 0), (0, BF16_ROWS - nf)))], axis=1).T.astype(BF16)
            bf = jnp.pad(b_forget[j].astype(F32), (0, BF16_ROWS - nf))[:, None]
            outs = _in_ab(
                x2, res, ng, w, wvt, _tile_gain(moba_q_gain[j], 8), _tile_gain(moba_k_gain[j], 8),
                _tile_gain(fox_q_gain[j], 8), _tile_gain(fox_k_gain[j], 8), bf, e256, tri, seq=s)
            if res is not None:
                x2, outs = outs[0], outs[1:]
            qa, ka, ga, qb, kb, gb, vt, km = outs
            ya = _guarded(_logit_bound(moba_q_gain[j], moba_k_gain[j]), functools.partial(_attn_ab, moba=True),
                          r3(qa), r3(ka), vt, r3(ga), km.reshape(b, 8, 512))
            yb = _guarded(_logit_bound(fox_q_gain[j], fox_k_gain[j]), functools.partial(_attn_ab, moba=False),
                          r3(qb), r3(kb), vt, r3(gb), None)
            res = (ya.reshape(b * s, 512), yb.reshape(b * s, 512), w_out_ab[j].astype(BF16))
        else:
            wj = w_in_cd[j]
            w = jnp.concatenate([wj[:, 0:1024], wj[:, 1536:2048], wj[:, 2048:2560][:, perm],
                                 wj[:, 2560:2688], wj[:, 2816:3328][:, perm]], axis=1).astype(BF16)
            wvt = jnp.concatenate([wj[:, 1024:1536], wj[:, 2688:2816]], axis=1).T.astype(BF16)
            outs = _in_cd(
                x2, res, ng, w, wvt, _tile_gain(diff_q_gain[j], 8), _tile_gain(diff_k_gain[j], 8),
                _tile_gain(swa_q_gain[j], 8), _tile_gain(swa_k_gain[j], 2), e256, seq=s)
            if res is not None:
                x2, outs = outs[0], outs[1:]
            qc, kc, gc, qd, kd, gd, vt = outs
            lam_init = 0.8 - 0.6 * math.exp(-0.3 * layer)
            yc = _guarded(_logit_bound(diff_q_gain[j], diff_k_gain[j]),
                          functools.partial(_attn_diff, lam_init=lam_init),
                          r3(qc), r3(kc), vt, r3(gc), diff_lambda[j].astype(F32),
                          diff_subln_gain[j].astype(F32)[None, :])
            sinks = swa_sinks[j].astype(F32).reshape(2, 4)
            swa_bound = jnp.maximum(_logit_bound(swa_q_gain[j], swa_k_gain[j]), jnp.max(sinks))
            yd = _guarded(swa_bound, _attn_swa,
                          r3(qd), r3(kd), vt, r3(gd), jnp.repeat(sinks, TS, axis=1))
            wo = w_out_cd[j]
            wo = jnp.concatenate([wo[:512], wo[512:][perm]], axis=0).astype(BF16)
            res = (yc.reshape(b * s, 512), yd.reshape(b * s, 512), wo)
    return _out_proj(x2, *res).reshape(b, s, d)
```
